```python
import jax, jax.numpy as jnp
from jax import lax
import numpy as np

D_MODEL = 2048
BATCH = 8
SEQ = 4096
DEPTH = 4

D_MIX = D_MODEL
D_CONV = D_MIX // 2
CONV_HEADS = 16
D_POOL = D_MIX - D_CONV
POOL_WINDOWS = (2, 4, 8, 16)
N_POOL_GROUPS = len(POOL_WINDOWS)
POOL_GROUP_DIM = D_POOL // N_POOL_GROUPS
CONV_WIDTH = 3
D_IN_PROJ = 3 * D_CONV + D_POOL
D_FF = 4 * D_MODEL
EPS = 1e-6

kernel_name = "hybrid_shortconv_pool_sqrelu_trunk"


def _rmsnorm(x, g):
    xf = x.astype(jnp.float32)
    r = lax.rsqrt(jnp.mean(xf * xf, axis=-1, keepdims=True) + EPS)
    return (xf * r).astype(x.dtype) * g


def _short_conv_mixer(b, c, xt, conv_w):
    u = c * xt
    s = u.shape[1]
    u_pad = jnp.pad(u, ((0, 0), (CONV_WIDTH - 1, 0), (0, 0)))
    conv = conv_w[0] * u_pad[:, 0:s] + conv_w[1] * u_pad[:, 1:s + 1] + conv_w[2] * u_pad[:, 2:s + 2]
    return b * conv


def _pool_mixer(v, pool_w, pool_scale):
    bsz, s, _ = v.shape
    vg = v.reshape(bsz, s, N_POOL_GROUPS, POOL_GROUP_DIM)
    csum = jnp.cumsum(vg.astype(jnp.float32), axis=1)
    pos = jnp.arange(s, dtype=jnp.float32)
    outs = []
    for g, w in enumerate(POOL_WINDOWS):
        cs = csum[:, :, g]
        lagged = jnp.pad(cs[:, :s - w], ((0, 0), (w, 0), (0, 0)))
        count = jnp.minimum(pos + 1.0, float(w))[None, :, None]
        mean = (cs - lagged) / count
        outs.append(mean.astype(v.dtype) - vg[:, :, g])
    d = jnp.stack(outs, axis=2)
    y = jnp.einsum('bsgc,gcd->bsgd', d, pool_w).reshape(bsz, s, D_POOL)
    return y * pool_scale


def setup_inputs(seed: int = 0) -> dict:
    key = jax.random.key(seed)
    ks = jax.random.split(key, 12)
    f32 = jnp.float32
    x = jax.random.normal(ks[0], (BATCH, SEQ, D_MODEL), f32)
    w_in = jax.random.normal(ks[1], (DEPTH, D_MODEL, D_IN_PROJ), f32) * D_MODEL ** -0.5
    conv_w = jax.random.normal(ks[2], (DEPTH, CONV_WIDTH, D_CONV), f32) * CONV_WIDTH ** -0.5
    pool_w = jax.random.normal(ks[3], (DEPTH, N_POOL_GROUPS, POOL_GROUP_DIM, POOL_GROUP_DIM), f32) * POOL_GROUP_DIM ** -0.5
    pool_scale = 1.0 + 0.1 * jax.random.normal(ks[4], (DEPTH, D_POOL), f32)
    w_out = jax.random.normal(ks[5], (DEPTH, D_MIX, D_MODEL), f32) * D_MIX ** -0.5
    norm_mix = 1.0 + 0.05 * jax.random.normal(ks[6], (DEPTH, D_MODEL), f32)
    norm_mlp = 1.0 + 0.05 * jax.random.normal(ks[7], (DEPTH, D_MODEL), f32)
    w_up = jax.random.normal(ks[8], (DEPTH, D_MODEL, D_FF), f32) * D_MODEL ** -0.5
    w_down = jax.random.normal(ks[9], (DEPTH, D_FF, D_MODEL), f32) * D_FF ** -0.5
    norm_final = 1.0 + 0.05 * jax.random.normal(ks[10], (D_MODEL,), f32)
    return {"x": x, "w_in": w_in, "conv_w": conv_w, "pool_w": pool_w, "pool_scale": pool_scale,
            "w_out": w_out, "norm_mix": norm_mix, "norm_mlp": norm_mlp, "w_up": w_up,
            "w_down": w_down, "norm_final": norm_final}


def reference(x, w_in, conv_w, pool_w, pool_scale, w_out, norm_mix, norm_mlp, w_up, w_down, norm_final):
    for l in range(DEPTH):
        h = _rmsnorm(x, norm_mix[l])
        proj = jnp.einsum('bsd,de->bse', h, w_in[l])
        b = proj[..., 0:D_CONV]
        c = proj[..., D_CONV:2 * D_CONV]
        xt = proj[..., 2 * D_CONV:3 * D_CONV]
        v = proj[..., 3 * D_CONV:]
        y_conv = _short_conv_mixer(b, c, xt, conv_w[l])
        y_pool = _pool_mixer(v, pool_w[l], pool_scale[l])
        y = jnp.concatenate([y_conv, y_pool], axis=-1)
        x = x + jnp.einsum('bse,ed->bsd', y, w_out[l])
        h = _rmsnorm(x, norm_mlp[l])
        u = jax.nn.relu(jnp.einsum('bsd,df->bsf', h, w_up[l]))
        x = x + jnp.einsum('bsf,fd->bsd', u * u, w_down[l])
    return _rmsnorm(x, norm_final)
```

```python
import functools

import jax
import jax.numpy as jnp
from jax import lax
from jax.experimental import pallas as pl
from jax.experimental.pallas import tpu as pltpu

EPS = 1e-6
CONV_WIDTH = 3
POOL_WINDOWS = (2, 4, 8, 16)
HALO_ROWS = 16
F32_SUBLANES = 8

VMEM_LIMIT_BYTES = 56 * 1024 * 1024

MIXER_TOKEN_TILE = 512
MLP_TOKEN_TILE = 512
MLP_FF_CHUNK = 1024


def _rmsnorm_f32(x, g):
    r = lax.rsqrt(jnp.mean(x * x, axis=-1, keepdims=True) + EPS)
    return (x * r) * g


def _dot(a, b):
    return jnp.dot(a, b, preferred_element_type=jnp.float32)


def _mixer_kernel(x_ref, g_ref, win_ref, cw_ref, pw_ref, ps_ref, wo_ref, o_ref,
                  h_ref, uhalo_ref, vhalo_ref, *, tiles_per_seq):
    i = pl.program_id(0)
    j = pl.program_id(1)
    tm = x_ref.shape[0]
    c = cw_ref.shape[1]

    @pl.when(j == 0)
    def _():
        x = x_ref[...]
        h_ref[...] = _rmsnorm_f32(x, g_ref[...]).astype(jnp.bfloat16)
        o_ref[...] = x

    tile_in_seq = i % tiles_per_seq
    seq_start = tile_in_seq == 0

    proj = _dot(h_ref[...], win_ref[...])
    b = proj[:, 0:c]
    u = proj[:, c:2 * c] * proj[:, 2 * c:3 * c]
    v = proj[:, 3 * c:4 * c]

    u_prev = jnp.where(seq_start, 0.0, uhalo_ref[j])
    uhalo_ref[j] = u[tm - HALO_ROWS:, :]
    ue = jnp.concatenate([u_prev, u], axis=0)
    cw = cw_ref[...]
    conv = (cw[0:1, :] * pltpu.roll(ue, 2, 0) + cw[1:2, :] * pltpu.roll(ue, 1, 0)
            + cw[2:3, :] * ue)
    y_conv = b * conv[HALO_ROWS:, :]

    v_prev = jnp.where(seq_start, 0.0, vhalo_ref[j])
    vhalo_ref[j] = v[tm - HALO_ROWS:, :]
    s = jnp.concatenate([v_prev, v], axis=0)
    for k in range(len(POOL_WINDOWS)):
        s = jnp.where(j >= k, s + pltpu.roll(s, 2 ** k, 0), s)
    pos = tile_in_seq * tm + lax.broadcasted_iota(jnp.int32, (tm, 1), 0)
    window = jnp.left_shift(2, j)
    count = jnp.minimum(pos + 1, window).astype(jnp.float32)
    mean = s[HALO_ROWS:, :] / count
    d = (mean - v).astype(jnp.bfloat16)
    y_pool = _dot(d, pw_ref[0]) * ps_ref[...]

    y = jnp.concatenate([y_conv, y_pool], axis=1).astype(jnp.bfloat16)
    o_ref[...] += _dot(y, wo_ref[...])


def _mlp_kernel(x_ref, g_ref, wu_ref, wd_ref, gf_ref, o_ref, h_ref, *, final_norm):
    k = pl.program_id(1)

    @pl.when(k == 0)
    def _():
        x = x_ref[...]
        h_ref[...] = _rmsnorm_f32(x, g_ref[...]).astype(jnp.bfloat16)
        o_ref[...] = x

    t = jnp.maximum(_dot(h_ref[...], wu_ref[...]), 0.0)
    o_ref[...] += _dot((t * t).astype(jnp.bfloat16), wd_ref[...])

    if final_norm:
        @pl.when(k == pl.num_programs(1) - 1)
        def _():
            o_ref[...] = _rmsnorm_f32(o_ref[...], gf_ref[...])


def _mixer_layer(x2, g, win, cw, pw, ps, wo, *, seq_len, token_tile):
    n_tok, d_model = x2.shape
    n_groups, group_dim, _ = pw.shape
    c = group_dim
    assert seq_len % token_tile == 0 and token_tile >= HALO_ROWS
    grid = (n_tok // token_tile, n_groups)
    kernel = functools.partial(_mixer_kernel, tiles_per_seq=seq_len // token_tile)
    return pl.pallas_call(
        kernel,
        grid=grid,
        in_specs=[
            pl.BlockSpec((token_tile, d_model), lambda i, j: (i, 0)),
            pl.BlockSpec((1, d_model), lambda i, j: (0, 0)),
            pl.BlockSpec((d_model, 4 * c), lambda i, j: (0, j)),
            pl.BlockSpec((CONV_WIDTH, c), lambda i, j: (0, j)),
            pl.BlockSpec((1, c, c), lambda i, j: (j, 0, 0)),
            pl.BlockSpec((1, c), lambda i, j: (0, j)),
            pl.BlockSpec((2 * c, d_model), lambda i, j: (j, 0)),
        ],
        out_specs=pl.BlockSpec((token_tile, d_model), lambda i, j: (i, 0)),
        out_shape=jax.ShapeDtypeStruct((n_tok, d_model), jnp.float32),
        scratch_shapes=[
            pltpu.VMEM((token_tile, d_model), jnp.bfloat16),
            pltpu.VMEM((n_groups, HALO_ROWS, c), jnp.float32),
            pltpu.VMEM((n_groups, HALO_ROWS, c), jnp.float32),
        ],
        compiler_params=pltpu.CompilerParams(
            dimension_semantics=("arbitrary", "arbitrary"),
            vmem_limit_bytes=VMEM_LIMIT_BYTES),
        name="mixer",
    )(x2, g, win, cw, pw, ps, wo)


def _mlp_layer(x2, g, wu, wd, gf, *, token_tile, ff_chunk, final_norm):
    n_tok, d_model = x2.shape
    d_ff = wu.shape[1]
    grid = (n_tok // token_tile, d_ff // ff_chunk)
    kernel = functools.partial(_mlp_kernel, final_norm=final_norm)
    return pl.pallas_call(
        kernel,
        grid=grid,
        in_specs=[
            pl.BlockSpec((token_tile, d_model), lambda i, k: (i, 0)),
            pl.BlockSpec((1, d_model), lambda i, k: (0, 0)),
            pl.BlockSpec((d_model, ff_chunk), lambda i, k: (0, k)),
            pl.BlockSpec((ff_chunk, d_model), lambda i, k: (k, 0)),
            pl.BlockSpec((1, d_model), lambda i, k: (0, 0)),
        ],
        out_specs=pl.BlockSpec((token_tile, d_model), lambda i, k: (i, 0)),
        out_shape=jax.ShapeDtypeStruct((n_tok, d_model), jnp.float32),
        scratch_shapes=[pltpu.VMEM((token_tile, d_model), jnp.bfloat16)],
        compiler_params=pltpu.CompilerParams(
            dimension_semantics=("arbitrary", "arbitrary"),
            vmem_limit_bytes=VMEM_LIMIT_BYTES),
        name="mlp_final" if final_norm else "mlp",
    )(x2, g, wu, wd, gf)


def _chunk_major_in(w_in, n_groups, c):
    d_model = w_in.shape[0]
    w = w_in.reshape(d_model, 4, n_groups, c).transpose(0, 2, 1, 3)
    return w.reshape(d_model, 4 * n_groups * c).astype(jnp.bfloat16)


def _chunk_major_out(w_out, n_groups, c):
    d_model = w_out.shape[1]
    w = w_out.reshape(2, n_groups, c, d_model).transpose(1, 0, 2, 3)
    return w.reshape(2 * n_groups * c, d_model).astype(jnp.bfloat16)


def kernel(x, w_in, conv_w, pool_w, pool_scale, w_out, norm_mix, norm_mlp, w_up, w_down, norm_final):
    batch, seq_len, d_model = x.shape
    depth = w_in.shape[0]
    n_groups, group_dim = pool_w.shape[1], pool_w.shape[2]
    d_conv = conv_w.shape[-1]
    assert n_groups == len(POOL_WINDOWS) and all(
        w == 2 ** (g + 1) for g, w in enumerate(POOL_WINDOWS))
    assert d_conv == n_groups * group_dim and w_in.shape[-1] == 4 * d_conv
    assert conv_w.shape[1] == CONV_WIDTH

    xs = x.reshape(batch * seq_len, d_model)
    gf = norm_final.reshape(1, d_model)
    for l in range(depth):
        win = _chunk_major_in(w_in[l], n_groups, group_dim)
        wo = _chunk_major_out(w_out[l], n_groups, group_dim)
        xs = _mixer_layer(
            xs, norm_mix[l].reshape(1, d_model), win, conv_w[l],
            pool_w[l].astype(jnp.bfloat16), pool_scale[l].reshape(1, d_conv), wo,
            seq_len=seq_len, token_tile=min(MIXER_TOKEN_TILE, seq_len))
        xs = _mlp_layer(
            xs, norm_mlp[l].reshape(1, d_model), w_up[l].astype(jnp.bfloat16),
            w_down[l].astype(jnp.bfloat16), gf,
            token_tile=min(MLP_TOKEN_TILE, batch * seq_len),
            ff_chunk=min(MLP_FF_CHUNK, w_up.shape[-1]), final_norm=(l == depth - 1))
    return xs.reshape(batch, seq_len, d_model)
```

```python
import functools

import jax
import jax.numpy as jnp
from jax import lax
from jax.experimental import pallas as pl
from jax.experimental.pallas import tpu as pltpu

EPS = 1e-6
CONV_WIDTH = 3
POOL_WINDOWS = (2, 4, 8, 16)
HALO_ROWS = 16

VMEM_LIMIT_BYTES = 60 * 1024 * 1024

MIXER_TOKEN_TILE = 512
MLP_TOKEN_TILE = 512
MLP_FF_CHUNK = 1024


def _rmsnorm_f32(x, g):
    r = lax.rsqrt(jnp.mean(x * x, axis=-1, keepdims=True) + EPS)
    return (x * r) * g


def _dot(a, b):
    return jnp.dot(a, b, preferred_element_type=jnp.float32)


def _mixer_kernel(xn_ref, g_ref, win_ref, cw_ref, pw_ref, ps_ref, wo_ref, o_ref,
                  h_ref, hn_ref, res_ref, uhalo_ref, vhalo_ref, *, tiles_per_seq):
    s = pl.program_id(0)
    tm = xn_ref.shape[0]
    n_groups, _, c = pw_ref.shape

    @pl.when(s == 0)
    def _():
        xn = xn_ref[...]
        h_ref[...] = _rmsnorm_f32(xn, g_ref[...]).astype(jnp.bfloat16)
        res_ref[...] = xn

    @pl.when(s > 0)
    def _():
        tile_in_seq = (s - 1) % tiles_per_seq
        seq_start = tile_in_seq == 0
        pos = tile_in_seq * tm + lax.broadcasted_iota(jnp.int32, (tm, 1), 0)

        def project(j):
            return _dot(h_ref[...], win_ref[:, 4 * c * j:4 * c * (j + 1)])

        proj_next = project(0)
        for j in range(n_groups):
            proj = proj_next
            if j + 1 < n_groups:
                proj_next = project(j + 1)
            b = proj[:, 0:c]
            u = proj[:, c:2 * c] * proj[:, 2 * c:3 * c]
            v = proj[:, 3 * c:4 * c]

            u_prev = jnp.where(seq_start, 0.0, uhalo_ref[j])
            uhalo_ref[j] = u[tm - HALO_ROWS:, :]
            ue = jnp.concatenate([u_prev, u], axis=0)
            cw = cw_ref[:, c * j:c * (j + 1)]
            conv = (cw[0:1, :] * pltpu.roll(ue, 2, 0) + cw[1:2, :] * pltpu.roll(ue, 1, 0)
                    + cw[2:3, :] * ue)
            y_conv = b * conv[HALO_ROWS:, :]

            v_prev = jnp.where(seq_start, 0.0, vhalo_ref[j])
            vhalo_ref[j] = v[tm - HALO_ROWS:, :]
            w = jnp.concatenate([v_prev, v], axis=0)
            for k in range(j + 1):
                w = w + pltpu.roll(w, 2 ** k, 0)
            count = jnp.minimum(pos + 1, POOL_WINDOWS[j]).astype(jnp.float32)
            mean = w[HALO_ROWS:, :] / count
            d = (mean - v).astype(jnp.bfloat16)
            y_pool = _dot(d, pw_ref[j]) * ps_ref[:, c * j:c * (j + 1)]

            y = jnp.concatenate([y_conv, y_pool], axis=1).astype(jnp.bfloat16)
            out_j = _dot(y, wo_ref[2 * c * j:2 * c * (j + 1), :])
            if j == 0:
                o_ref[...] = res_ref[...] + out_j
            else:
                o_ref[...] += out_j

        xn = xn_ref[...]
        hn_ref[...] = _rmsnorm_f32(xn, g_ref[...]).astype(jnp.bfloat16)
        res_ref[...] = xn
        h_ref[...] = hn_ref[...]


def _mlp_kernel(x_ref, g_ref, wu_ref, wd_ref, gf_ref, o_ref, h_ref, *, final_norm):
    k = pl.program_id(1)

    @pl.when(k == 0)
    def _():
        x = x_ref[...]
        h_ref[...] = _rmsnorm_f32(x, g_ref[...]).astype(jnp.bfloat16)
        o_ref[...] = x

    t = jnp.maximum(_dot(h_ref[...], wu_ref[...]), 0.0)
    o_ref[...] += _dot((t * t).astype(jnp.bfloat16), wd_ref[...])

    if final_norm:
        @pl.when(k == pl.num_programs(1) - 1)
        def _():
            o_ref[...] = _rmsnorm_f32(o_ref[...], gf_ref[...])


def _resident(block_shape, index_map):
    return pl.BlockSpec(block_shape, index_map, pipeline_mode=pl.Buffered(1))


def _mixer_layer(x2, g, win, cw, pw, ps, wo, *, layer, seq_len, token_tile):
    n_tok, d_model = x2.shape
    _, n_groups, c, _ = pw.shape
    d_conv = n_groups * c
    assert seq_len % token_tile == 0 and token_tile >= HALO_ROWS
    n_tiles = n_tok // token_tile
    kernel = functools.partial(_mixer_kernel, tiles_per_seq=seq_len // token_tile)
    return pl.pallas_call(
        kernel,
        grid=(n_tiles + 1,),
        in_specs=[
            pl.BlockSpec((token_tile, d_model), lambda s: (jnp.minimum(s, n_tiles - 1), 0)),
            _resident((None, 1, d_model), lambda s: (layer, 0, 0)),
            _resident((None, d_model, 4 * d_conv), lambda s: (layer, 0, 0)),
            _resident((None, CONV_WIDTH, d_conv), lambda s: (layer, 0, 0)),
            _resident((None, n_groups, c, c), lambda s: (layer, 0, 0, 0)),
            _resident((None, 1, d_conv), lambda s: (layer, 0, 0)),
            _resident((None, 2 * d_conv, d_model), lambda s: (layer, 0, 0)),
        ],
        out_specs=pl.BlockSpec((token_tile, d_model), lambda s: (jnp.maximum(s - 1, 0), 0)),
        out_shape=jax.ShapeDtypeStruct((n_tok, d_model), jnp.float32),
        scratch_shapes=[
            pltpu.VMEM((token_tile, d_model), jnp.bfloat16),
            pltpu.VMEM((token_tile, d_model), jnp.bfloat16),
            pltpu.VMEM((token_tile, d_model), jnp.float32),
            pltpu.VMEM((n_groups, HALO_ROWS, c), jnp.float32),
            pltpu.VMEM((n_groups, HALO_ROWS, c), jnp.float32),
        ],
        compiler_params=pltpu.CompilerParams(
            dimension_semantics=("arbitrary",),
            vmem_limit_bytes=VMEM_LIMIT_BYTES),
        name="mixer",
    )(x2, g, win, cw, pw, ps, wo)


def _mlp_layer(x2, g, wu, wd, gf, *, layer, token_tile, ff_chunk, final_norm):
    n_tok, d_model = x2.shape
    d_ff = wu.shape[-1]
    grid = (n_tok // token_tile, d_ff // ff_chunk)
    kernel = functools.partial(_mlp_kernel, final_norm=final_norm)
    return pl.pallas_call(
        kernel,
        grid=grid,
        in_specs=[
            pl.BlockSpec((token_tile, d_model), lambda i, k: (i, 0)),
            pl.BlockSpec((None, 1, d_model), lambda i, k: (layer, 0, 0)),
            pl.BlockSpec((None, d_model, ff_chunk), lambda i, k: (layer, 0, k)),
            pl.BlockSpec((None, ff_chunk, d_model), lambda i, k: (layer, k, 0)),
            pl.BlockSpec((1, d_model), lambda i, k: (0, 0)),
        ],
        out_specs=pl.BlockSpec((token_tile, d_model), lambda i, k: (i, 0)),
        out_shape=jax.ShapeDtypeStruct((n_tok, d_model), jnp.float32),
        scratch_shapes=[pltpu.VMEM((token_tile, d_model), jnp.bfloat16)],
        compiler_params=pltpu.CompilerParams(
            dimension_semantics=("arbitrary", "arbitrary"),
            vmem_limit_bytes=VMEM_LIMIT_BYTES),
        name="mlp_final" if final_norm else "mlp",
    )(x2, g, wu, wd, gf)


def _chunk_major_in(w_in, n_groups, c):
    depth, d_model, _ = w_in.shape
    w = w_in.astype(jnp.bfloat16).reshape(depth, d_model, 4, n_groups, c).transpose(0, 1, 3, 2, 4)
    return w.reshape(depth, d_model, 4 * n_groups * c)


def _chunk_major_out(w_out, n_groups, c):
    depth, _, d_model = w_out.shape
    w = w_out.astype(jnp.bfloat16).reshape(depth, 2, n_groups, c, d_model).transpose(0, 2, 1, 3, 4)
    return w.reshape(depth, 2 * n_groups * c, d_model)


def kernel(x, w_in, conv_w, pool_w, pool_scale, w_out, norm_mix, norm_mlp, w_up, w_down, norm_final):
    batch, seq_len, d_model = x.shape
    depth = w_in.shape[0]
    n_groups, group_dim = pool_w.shape[1], pool_w.shape[2]
    d_conv = conv_w.shape[-1]
    assert n_groups == len(POOL_WINDOWS) and all(
        w == 2 ** (g + 1) for g, w in enumerate(POOL_WINDOWS))
    assert d_conv == n_groups * group_dim and w_in.shape[-1] == 4 * d_conv
    assert conv_w.shape[1] == CONV_WIDTH

    win = _chunk_major_in(w_in, n_groups, group_dim)
    wo = _chunk_major_out(w_out, n_groups, group_dim)
    pw = pool_w.astype(jnp.bfloat16)
    wu = w_up.astype(jnp.bfloat16)
    wd = w_down.astype(jnp.bfloat16)
    g_mix = norm_mix.reshape(depth, 1, d_model)
    g_mlp = norm_mlp.reshape(depth, 1, d_model)
    ps = pool_scale.reshape(depth, 1, d_conv)
    gf = norm_final.reshape(1, d_model)

    xs = x.reshape(batch * seq_len, d_model)
    for l in range(depth):
        xs = _mixer_layer(xs, g_mix, win, conv_w, pw, ps, wo, layer=l, seq_len=seq_len,
                          token_tile=min(MIXER_TOKEN_TILE, seq_len))
        xs = _mlp_layer(xs, g_mlp, wu, wd, gf, layer=l,
                        token_tile=min(MLP_TOKEN_TILE, batch * seq_len),
                        ff_chunk=min(MLP_FF_CHUNK, w_up.shape[-1]), final_norm=(l == depth - 1))
    return xs.reshape(batch, seq_len, d_model)
```

```python
import functools

import jax
import jax.numpy as jnp
from jax import lax
from jax.experimental import pallas as pl
from jax.experimental.pallas import tpu as pltpu

EPS = 1e-6
CONV_WIDTH = 3
POOL_WINDOWS = (2, 4, 8, 16)
HALO_ROWS = 16
BF16_SUBLANES = 16

VMEM_LIMIT_BYTES = 60 * 1024 * 1024

MIXER_TOKEN_TILE = 512
MLP_TOKEN_TILE = 512
MLP_FF_CHUNK = 1024


def _rmsnorm_f32(x, g):
    r = lax.rsqrt(jnp.mean(x * x, axis=-1, keepdims=True) + EPS)
    return (x * r) * g


def _dot(a, b):
    return jnp.dot(a, b, preferred_element_type=jnp.float32)


def _mixer_kernel(xn_ref, g_ref, win_ref, cw_ref, pw_ref, ps_ref, wo_ref, o_ref,
                  h_ref, res_ref, plo_ref, phi_ref, uhalo_ref, vhalo_ref, *, tiles_per_seq):
    s = pl.program_id(0)
    tm = xn_ref.shape[0]
    n_groups, _, c = pw_ref.shape

    def project(j):
        lo = _dot(h_ref[...], win_ref[:, 4 * c * j:4 * c * j + 2 * c])
        hi = _dot(h_ref[...], win_ref[:, 4 * c * j + 2 * c:4 * c * (j + 1)])
        return lo, hi

    @pl.when(s == 0)
    def _():
        xn = xn_ref[...]
        h_ref[...] = _rmsnorm_f32(xn, g_ref[...]).astype(jnp.bfloat16)
        res_ref[...] = xn
        plo_ref[...], phi_ref[...] = project(0)

    @pl.when(s > 0)
    def _():
        tile_in_seq = (s - 1) % tiles_per_seq
        seq_start = tile_in_seq == 0
        pos = tile_in_seq * tm + lax.broadcasted_iota(jnp.int32, (tm, 1), 0)

        lo, hi = plo_ref[...], phi_ref[...]
        for j in range(n_groups):
            v = lo[:, 0:c]
            v_prev = jnp.where(seq_start, 0.0, vhalo_ref[j])
            vhalo_ref[j] = v[tm - HALO_ROWS:, :]
            w = jnp.concatenate([v_prev, v], axis=0)
            for k in range(j + 1):
                w = w + pltpu.roll(w, 2 ** k, 0)
            count = jnp.minimum(pos + 1, POOL_WINDOWS[j]).astype(jnp.float32)
            mean = w[HALO_ROWS:, :] / count
            d = (mean - v).astype(jnp.bfloat16)
            pooled = _dot(d, pw_ref[j])

            if j + 1 < n_groups:
                lo_next, hi_next = project(j + 1)
            else:
                h_ref[...] = _rmsnorm_f32(xn_ref[...], g_ref[...]).astype(jnp.bfloat16)
                plo_ref[...], phi_ref[...] = project(0)

            u = lo[:, c:2 * c] * hi[:, 0:c]
            b = hi[:, c:2 * c]
            u_prev = jnp.where(seq_start, 0.0, uhalo_ref[j])
            uhalo_ref[j] = u[tm - HALO_ROWS:, :]
            ue = jnp.concatenate([u_prev, u], axis=0)
            cw = cw_ref[:, c * j:c * (j + 1)]
            conv = (cw[0:1, :] * pltpu.roll(ue, 2, 0) + cw[1:2, :] * pltpu.roll(ue, 1, 0)
                    + cw[2:3, :] * ue)
            y_conv = b * conv[HALO_ROWS:, :]
            y_pool = pooled * ps_ref[:, c * j:c * (j + 1)]

            y = jnp.concatenate([y_conv, y_pool], axis=1).astype(jnp.bfloat16)
            out_j = _dot(y, wo_ref[2 * c * j:2 * c * (j + 1), :])
            if j == 0:
                o_ref[...] = res_ref[...] + out_j
                res_ref[...] = xn_ref[...]
            else:
                o_ref[...] += out_j
            if j + 1 < n_groups:
                lo, hi = lo_next, hi_next


def _mlp_kernel(x_ref, xn_ref, g_ref, wu_ref, wdp_ref, wdc_ref, gf_ref, o_ref, h_ref, hn_ref, u_ref, *,
                final_norm):
    i = pl.program_id(0)
    k = pl.program_id(1)
    nk = pl.num_programs(1) - 1
    f = wdc_ref.shape[0]

    def up(half):
        t = jnp.maximum(_dot(h_ref[...], wu_ref[:, half * f:(half + 1) * f]), 0.0)
        return (t * t).astype(jnp.bfloat16)

    def norm_next_rows():
        n = x_ref.shape[0] // nk
        rows = pl.ds(pl.multiple_of(k * n, n), n)
        hn_ref[rows, :] = _rmsnorm_f32(xn_ref[rows, :], g_ref[...]).astype(jnp.bfloat16)

    @pl.when(k == 0)
    def _():
        @pl.when(i == 0)
        def _():
            h_ref[...] = _rmsnorm_f32(x_ref[...], g_ref[...]).astype(jnp.bfloat16)

        u0 = up(0)
        u1 = up(1)
        norm_next_rows()
        o_ref[...] = x_ref[...] + _dot(u0, wdc_ref[...])
        u_ref[...] = u1

    @pl.when((k > 0) & (k < nk))
    def _():
        u0 = up(0)
        o_ref[...] += _dot(u_ref[...], wdp_ref[...])
        u1 = up(1)
        norm_next_rows()
        o_ref[...] += _dot(u0, wdc_ref[...])
        u_ref[...] = u1

    @pl.when(k == nk)
    def _():
        o = o_ref[...] + _dot(u_ref[...], wdp_ref[...])
        h_ref[...] = hn_ref[...]
        if final_norm:
            o = _rmsnorm_f32(o, gf_ref[...])
        o_ref[...] = o


def _resident(block_shape, index_map):
    return pl.BlockSpec(block_shape, index_map, pipeline_mode=pl.Buffered(1))


def _mixer_layer(x2, g, win, cw, pw, ps, wo, *, layer, seq_len, token_tile):
    n_tok, d_model = x2.shape
    _, n_groups, c, _ = pw.shape
    d_conv = n_groups * c
    assert seq_len % token_tile == 0 and token_tile >= HALO_ROWS
    n_tiles = n_tok // token_tile
    kernel = functools.partial(_mixer_kernel, tiles_per_seq=seq_len // token_tile)
    return pl.pallas_call(
        kernel,
        grid=(n_tiles + 1,),
        in_specs=[
            pl.BlockSpec((token_tile, d_model), lambda s: (jnp.minimum(s, n_tiles - 1), 0)),
            _resident((None, 1, d_model), lambda s: (layer, 0, 0)),
            _resident((None, d_model, 4 * d_conv), lambda s: (layer, 0, 0)),
            _resident((None, CONV_WIDTH, d_conv), lambda s: (layer, 0, 0)),
            _resident((None, n_groups, c, c), lambda s: (layer, 0, 0, 0)),
            _resident((None, 1, d_conv), lambda s: (layer, 0, 0)),
            _resident((None, 2 * d_conv, d_model), lambda s: (layer, 0, 0)),
        ],
        out_specs=pl.BlockSpec((token_tile, d_model), lambda s: (jnp.maximum(s - 1, 0), 0)),
        out_shape=jax.ShapeDtypeStruct((n_tok, d_model), jnp.float32),
        scratch_shapes=[
            pltpu.VMEM((token_tile, d_model), jnp.bfloat16),
            pltpu.VMEM((token_tile, d_model), jnp.float32),
            pltpu.VMEM((token_tile, 2 * c), jnp.float32),
            pltpu.VMEM((token_tile, 2 * c), jnp.float32),
            pltpu.VMEM((n_groups, HALO_ROWS, c), jnp.float32),
            pltpu.VMEM((n_groups, HALO_ROWS, c), jnp.float32),
        ],
        compiler_params=pltpu.CompilerParams(
            dimension_semantics=("arbitrary",),
            vmem_limit_bytes=VMEM_LIMIT_BYTES),
        name="mixer",
    )(x2, g, win, cw, pw, ps, wo)


def _mlp_layer(x2, g, wu, wd, gf, *, layer, token_tile, ff_chunk, final_norm):
    n_tok, d_model = x2.shape
    d_ff = wu.shape[-1]
    n_tiles, nk = n_tok // token_tile, d_ff // ff_chunk
    half = ff_chunk // 2
    assert token_tile % nk == 0 and (token_tile // nk) % BF16_SUBLANES == 0
    kernel = functools.partial(_mlp_kernel, final_norm=final_norm)
    return pl.pallas_call(
        kernel,
        grid=(n_tiles, nk + 1),
        in_specs=[
            pl.BlockSpec((token_tile, d_model), lambda i, k: (i, 0)),
            pl.BlockSpec((token_tile, d_model), lambda i, k: (jnp.minimum(i + 1, n_tiles - 1), 0)),
            pl.BlockSpec((None, 1, d_model), lambda i, k: (layer, 0, 0)),
            pl.BlockSpec((None, d_model, ff_chunk), lambda i, k: (layer, 0, jnp.minimum(k, nk - 1))),
            pl.BlockSpec((None, half, d_model), lambda i, k: (layer, jnp.maximum(2 * k - 1, 1), 0)),
            pl.BlockSpec((None, half, d_model),
                         lambda i, k: (layer, jnp.minimum(2 * k, 2 * nk - 2), 0)),
            pl.BlockSpec((1, d_model), lambda i, k: (0, 0)),
        ],
        out_specs=pl.BlockSpec((token_tile, d_model), lambda i, k: (i, 0)),
        out_shape=jax.ShapeDtypeStruct((n_tok, d_model), jnp.float32),
        scratch_shapes=[
            pltpu.VMEM((token_tile, d_model), jnp.bfloat16),
            pltpu.VMEM((token_tile, d_model), jnp.bfloat16),
            pltpu.VMEM((token_tile, half), jnp.bfloat16),
        ],
        compiler_params=pltpu.CompilerParams(
            dimension_semantics=("arbitrary", "arbitrary"),
            vmem_limit_bytes=VMEM_LIMIT_BYTES),
        name="mlp_final" if final_norm else "mlp",
    )(x2, x2, g, wu, wd, wd, gf)


def _relayout_kernel(w_ref, o_ref):
    o_ref[...] = w_ref[...].astype(o_ref.dtype)


def _chunk_major_in(w_in, n_groups, c):
    depth, d_model, width = w_in.shape
    n_types = width // (n_groups * c)
    return pl.pallas_call(
        _relayout_kernel,
        grid=(depth, n_types, n_groups),
        in_specs=[pl.BlockSpec((None, d_model, c), lambda l, t, j: (l, 0, t * n_groups + j))],
        out_specs=pl.BlockSpec((None, d_model, c), lambda l, t, j: (l, 0, j * n_types + n_types - 1 - t)),
        out_shape=jax.ShapeDtypeStruct(w_in.shape, jnp.bfloat16),
        name="relayout_in",
    )(w_in)


def _chunk_major_out(w_out, n_groups, c):
    depth, rows, d_model = w_out.shape
    n_types = rows // (n_groups * c)
    return pl.pallas_call(
        _relayout_kernel,
        grid=(depth, n_types, n_groups),
        in_specs=[pl.BlockSpec((None, c, d_model), lambda l, t, j: (l, t * n_groups + j, 0))],
        out_specs=pl.BlockSpec((None, c, d_model), lambda l, t, j: (l, j * n_types + t, 0)),
        out_shape=jax.ShapeDtypeStruct(w_out.shape, jnp.bfloat16),
        name="relayout_out",
    )(w_out)


def kernel(x, w_in, conv_w, pool_w, pool_scale, w_out, norm_mix, norm_mlp, w_up, w_down, norm_final):
    batch, seq_len, d_model = x.shape
    depth = w_in.shape[0]
    n_groups, group_dim = pool_w.shape[1], pool_w.shape[2]
    d_conv = conv_w.shape[-1]
    assert n_groups == len(POOL_WINDOWS) and all(
        w == 2 ** (g + 1) for g, w in enumerate(POOL_WINDOWS))
    assert d_conv == n_groups * group_dim and w_in.shape[-1] == 4 * d_conv
    assert conv_w.shape[1] == CONV_WIDTH

    win = _chunk_major_in(w_in, n_groups, group_dim)
    wo = _chunk_major_out(w_out, n_groups, group_dim)
    pw = pool_w.astype(jnp.bfloat16)
    wu = w_up.astype(jnp.bfloat16)
    wd = w_down.astype(jnp.bfloat16)
    g_mix = norm_mix.reshape(depth, 1, d_model)
    g_mlp = norm_mlp.reshape(depth, 1, d_model)
    ps = pool_scale.reshape(depth, 1, d_conv)
    gf = norm_final.reshape(1, d_model)

    xs = x.reshape(batch * seq_len, d_model)
    for l in range(depth):
        xs = _mixer_layer(xs, g_mix, win, conv_w, pw, ps, wo, layer=l, seq_len=seq_len,
                          token_tile=min(MIXER_TOKEN_TILE, seq_len))
        xs = _mlp_layer(xs, g_mlp, wu, wd, gf, layer=l,
                        token_tile=min(MLP_TOKEN_TILE, batch * seq_len),
                        ff_chunk=min(MLP_FF_CHUNK, w_up.shape[-1]), final_norm=(l == depth - 1))
    return xs.reshape(batch, seq_len, d_model)
```

```python
import functools

import jax
import jax.numpy as jnp
from jax import lax
from jax.experimental import pallas as pl
from jax.experimental.pallas import tpu as pltpu

EPS = 1e-6
CONV_WIDTH = 3
POOL_WINDOWS = (2, 4, 8, 16)
HALO_ROWS = 16

VMEM_LIMIT_BYTES = 60 * 1024 * 1024

MIXER_TOKEN_TILE = 512
MLP_TOKEN_TILE = 512
MLP_FF_CHUNK = 1024


def _rmsnorm_f32(x, g):
    r = lax.rsqrt(jnp.mean(x * x, axis=-1, keepdims=True) + EPS)
    return (x * r) * g


def _dot(a, b):
    return jnp.dot(a, b, preferred_element_type=jnp.float32)


def _mixer_kernel(xn_ref, g_ref, win_ref, cw_ref, pw_ref, ps_ref, wo_ref, o_ref,
                  h_ref, res_ref, plo_ref, phi_ref, uhalo_ref, vhalo_ref, *, tiles_per_seq):
    s = pl.program_id(0)
    tm = xn_ref.shape[0]
    n_groups, _, c = pw_ref.shape

    def project(j):
        lo = _dot(h_ref[...], win_ref[:, 4 * c * j:4 * c * j + 2 * c])
        hi = _dot(h_ref[...], win_ref[:, 4 * c * j + 2 * c:4 * c * (j + 1)])
        return lo, hi

    @pl.when(s == 0)
    def _():
        xn = xn_ref[...]
        h_ref[...] = _rmsnorm_f32(xn, g_ref[...]).astype(jnp.bfloat16)
        res_ref[...] = xn
        plo_ref[...], phi_ref[...] = project(0)

    @pl.when(s > 0)
    def _():
        tile_in_seq = (s - 1) % tiles_per_seq
        seq_start = tile_in_seq == 0
        pos = tile_in_seq * tm + lax.broadcasted_iota(jnp.int32, (tm, 1), 0)

        lo, hi = plo_ref[...], phi_ref[...]
        for j in range(n_groups):
            v = lo[:, 0:c]
            v_prev = jnp.where(seq_start, 0.0, vhalo_ref[j])
            vhalo_ref[j] = v[tm - HALO_ROWS:, :]
            w = jnp.concatenate([v_prev, v], axis=0)
            for k in range(j + 1):
                w = w + pltpu.roll(w, 2 ** k, 0)
            count = jnp.minimum(pos + 1, POOL_WINDOWS[j]).astype(jnp.float32)
            mean = w[HALO_ROWS:, :] / count
            d = (mean - v).astype(jnp.bfloat16)
            pooled = _dot(d, pw_ref[j])

            if j + 1 < n_groups:
                lo_next, hi_next = project(j + 1)
            else:
                h_ref[...] = _rmsnorm_f32(xn_ref[...], g_ref[...]).astype(jnp.bfloat16)
                plo_ref[...], phi_ref[...] = project(0)

            u = lo[:, c:2 * c] * hi[:, 0:c]
            b = hi[:, c:2 * c]
            u_prev = jnp.where(seq_start, 0.0, uhalo_ref[j])
            uhalo_ref[j] = u[tm - HALO_ROWS:, :]
            ue = jnp.concatenate([u_prev, u], axis=0)
            cw = cw_ref[:, c * j:c * (j + 1)]
            conv = (cw[0:1, :] * pltpu.roll(ue, 2, 0) + cw[1:2, :] * pltpu.roll(ue, 1, 0)
                    + cw[2:3, :] * ue)
            y_conv = b * conv[HALO_ROWS:, :]
            y_pool = pooled * ps_ref[:, c * j:c * (j + 1)]

            y = jnp.concatenate([y_conv, y_pool], axis=1).astype(jnp.bfloat16)
            out_j = _dot(y, wo_ref[2 * c * j:2 * c * (j + 1), :])
            if j == 0:
                o_ref[...] = res_ref[...] + out_j
                res_ref[...] = xn_ref[...]
            else:
                o_ref[...] += out_j
            if j + 1 < n_groups:
                lo, hi = lo_next, hi_next


def _mlp_kernel(x_ref, g_ref, wu_ref, wd_ref, gf_ref, o_ref, h_ref, *, final_norm):
    k = pl.program_id(1)

    @pl.when(k == 0)
    def _():
        x = x_ref[...]
        h_ref[...] = _rmsnorm_f32(x, g_ref[...]).astype(jnp.bfloat16)
        o_ref[...] = x

    t = jnp.maximum(_dot(h_ref[...], wu_ref[...]), 0.0)
    o_ref[...] += _dot((t * t).astype(jnp.bfloat16), wd_ref[...])

    if final_norm:
        @pl.when(k == pl.num_programs(1) - 1)
        def _():
            o_ref[...] = _rmsnorm_f32(o_ref[...], gf_ref[...])


def _resident(block_shape, index_map):
    return pl.BlockSpec(block_shape, index_map, pipeline_mode=pl.Buffered(1))


def _mixer_layer(x2, g, win, cw, pw, ps, wo, *, layer, seq_len, token_tile):
    n_tok, d_model = x2.shape
    _, n_groups, c, _ = pw.shape
    d_conv = n_groups * c
    assert seq_len % token_tile == 0 and token_tile >= HALO_ROWS
    n_tiles = n_tok // token_tile
    kernel = functools.partial(_mixer_kernel, tiles_per_seq=seq_len // token_tile)
    return pl.pallas_call(
        kernel,
        grid=(n_tiles + 1,),
        in_specs=[
            pl.BlockSpec((token_tile, d_model), lambda s: (jnp.minimum(s, n_tiles - 1), 0)),
            _resident((None, 1, d_model), lambda s: (layer, 0, 0)),
            _resident((None, d_model, 4 * d_conv), lambda s: (layer, 0, 0)),
            _resident((None, CONV_WIDTH, d_conv), lambda s: (layer, 0, 0)),
            _resident((None, n_groups, c, c), lambda s: (layer, 0, 0, 0)),
            _resident((None, 1, d_conv), lambda s: (layer, 0, 0)),
            _resident((None, 2 * d_conv, d_model), lambda s: (layer, 0, 0)),
        ],
        out_specs=pl.BlockSpec((token_tile, d_model), lambda s: (jnp.maximum(s - 1, 0), 0)),
        out_shape=jax.ShapeDtypeStruct((n_tok, d_model), jnp.float32),
        scratch_shapes=[
            pltpu.VMEM((token_tile, d_model), jnp.bfloat16),
            pltpu.VMEM((token_tile, d_model), jnp.float32),
            pltpu.VMEM((token_tile, 2 * c), jnp.float32),
            pltpu.VMEM((token_tile, 2 * c), jnp.float32),
            pltpu.VMEM((n_groups, HALO_ROWS, c), jnp.float32),
            pltpu.VMEM((n_groups, HALO_ROWS, c), jnp.float32),
        ],
        compiler_params=pltpu.CompilerParams(
            dimension_semantics=("arbitrary",),
            vmem_limit_bytes=VMEM_LIMIT_BYTES),
        name="mixer",
    )(x2, g, win, cw, pw, ps, wo)


def _mlp_layer(x2, g, wu, wd, gf, *, layer, token_tile, ff_chunk, final_norm):
    n_tok, d_model = x2.shape
    d_ff = wu.shape[-1]
    grid = (n_tok // token_tile, d_ff // ff_chunk)
    kernel = functools.partial(_mlp_kernel, final_norm=final_norm)
    return pl.pallas_call(
        kernel,
        grid=grid,
        in_specs=[
            pl.BlockSpec((token_tile, d_model), lambda i, k: (i, 0)),
            pl.BlockSpec((None, 1, d_model), lambda i, k: (layer, 0, 0)),
            pl.BlockSpec((None, d_model, ff_chunk), lambda i, k: (layer, 0, k)),
            pl.BlockSpec((None, ff_chunk, d_model), lambda i, k: (layer, k, 0)),
            pl.BlockSpec((1, d_model), lambda i, k: (0, 0)),
        ],
        out_specs=pl.BlockSpec((token_tile, d_model), lambda i, k: (i, 0)),
        out_shape=jax.ShapeDtypeStruct((n_tok, d_model), jnp.float32),
        scratch_shapes=[pltpu.VMEM((token_tile, d_model), jnp.bfloat16)],
        compiler_params=pltpu.CompilerParams(
            dimension_semantics=("arbitrary", "arbitrary"),
            vmem_limit_bytes=VMEM_LIMIT_BYTES),
        name="mlp_final" if final_norm else "mlp",
    )(x2, g, wu, wd, gf)


def _relayout_kernel(w_ref, o_ref):
    o_ref[...] = w_ref[...].astype(o_ref.dtype)


def _chunk_major_in(w_in, n_groups, c):
    depth, d_model, width = w_in.shape
    n_types = width // (n_groups * c)
    return pl.pallas_call(
        _relayout_kernel,
        grid=(depth, n_types, n_groups),
        in_specs=[pl.BlockSpec((None, d_model, c), lambda l, t, j: (l, 0, t * n_groups + j))],
        out_specs=pl.BlockSpec((None, d_model, c), lambda l, t, j: (l, 0, j * n_types + n_types - 1 - t)),
        out_shape=jax.ShapeDtypeStruct(w_in.shape, jnp.bfloat16),
        name="relayout_in",
    )(w_in)


def _chunk_major_out(w_out, n_groups, c):
    depth, rows, d_model = w_out.shape
    n_types = rows // (n_groups * c)
    return pl.pallas_call(
        _relayout_kernel,
        grid=(depth, n_types, n_groups),
        in_specs=[pl.BlockSpec((None, c, d_model), lambda l, t, j: (l, t * n_groups + j, 0))],
        out_specs=pl.BlockSpec((None, c, d_model), lambda l, t, j: (l, j * n_types + t, 0)),
        out_shape=jax.ShapeDtypeStruct(w_out.shape, jnp.bfloat16),
        name="relayout_out",
    )(w_out)


def kernel(x, w_in, conv_w, pool_w, pool_scale, w_out, norm_mix, norm_mlp, w_up, w_down, norm_final):
    batch, seq_len, d_model = x.shape
    depth = w_in.shape[0]
    n_groups, group_dim = pool_w.shape[1], pool_w.shape[2]
    d_conv = conv_w.shape[-1]
    assert n_groups == len(POOL_WINDOWS) and all(
        w == 2 ** (g + 1) for g, w in enumerate(POOL_WINDOWS))
    assert d_conv == n_groups * group_dim and w_in.shape[-1] == 4 * d_conv
    assert conv_w.shape[1] == CONV_WIDTH

    win = _chunk_major_in(w_in, n_groups, group_dim)
    wo = _chunk_major_out(w_out, n_groups, group_dim)
    pw = pool_w.astype(jnp.bfloat16)
    wu = w_up.astype(jnp.bfloat16)
    wd = w_down.astype(jnp.bfloat16)
    g_mix = norm_mix.reshape(depth, 1, d_model)
    g_mlp = norm_mlp.reshape(depth, 1, d_model)
    ps = pool_scale.reshape(depth, 1, d_conv)
    gf = norm_final.reshape(1, d_model)

    xs = x.reshape(batch * seq_len, d_model)
    for l in range(depth):
        xs = _mixer_layer(xs, g_mix, win, conv_w, pw, ps, wo, layer=l, seq_len=seq_len,
                          token_tile=min(MIXER_TOKEN_TILE, seq_len))
        xs = _mlp_layer(xs, g_mlp, wu, wd, gf, layer=l,
                        token_tile=min(MLP_TOKEN_TILE, batch * seq_len),
                        ff_chunk=min(MLP_FF_CHUNK, w_up.shape[-1]), final_norm=(l == depth - 1))
    return xs.reshape(batch, seq_len, d_model)
```

```python
import functools

import jax
import jax.numpy as jnp
from jax import lax
from jax.experimental import pallas as pl
from jax.experimental.pallas import tpu as pltpu

EPS = 1e-6
CONV_WIDTH = 3
POOL_WINDOWS = (2, 4, 8, 16)
HALO_ROWS = 16
BF16_SUBLANES = 16

VMEM_LIMIT_BYTES = 60 * 1024 * 1024

MIXER_TOKEN_TILE = 512
MLP_TOKEN_TILE = 512
MLP_FF_CHUNK = 2048


def _rmsnorm_f32(x, g):
    r = lax.rsqrt(jnp.mean(x * x, axis=-1, keepdims=True) + EPS)
    return (x * r) * g


def _dot(a, b):
    return jnp.dot(a, b, preferred_element_type=jnp.float32)


def _mixer_kernel(xn_ref, g_ref, win_ref, cw_ref, pw_ref, ps_ref, wo_ref, o_ref,
                  h_ref, res_ref, plo_ref, phi_ref, uhalo_ref, vhalo_ref, *, tiles_per_seq):
    s = pl.program_id(0)
    tm = xn_ref.shape[0]
    n_groups, _, c = pw_ref.shape

    def project(j):
        lo = _dot(h_ref[...], win_ref[:, 4 * c * j:4 * c * j + 2 * c])
        hi = _dot(h_ref[...], win_ref[:, 4 * c * j + 2 * c:4 * c * (j + 1)])
        return lo, hi

    @pl.when(s == 0)
    def _():
        xn = xn_ref[...]
        h_ref[...] = _rmsnorm_f32(xn, g_ref[...]).astype(jnp.bfloat16)
        res_ref[...] = xn
        plo_ref[...], phi_ref[...] = project(0)

    @pl.when(s > 0)
    def _():
        tile_in_seq = (s - 1) % tiles_per_seq
        seq_start = tile_in_seq == 0
        pos = tile_in_seq * tm + lax.broadcasted_iota(jnp.int32, (tm, 1), 0)

        lo, hi = plo_ref[...], phi_ref[...]
        for j in range(n_groups):
            v = lo[:, 0:c]
            v_prev = jnp.where(seq_start, 0.0, vhalo_ref[j])
            vhalo_ref[j] = v[tm - HALO_ROWS:, :]
            w = jnp.concatenate([v_prev, v], axis=0)
            for k in range(j + 1):
                w = w + pltpu.roll(w, 2 ** k, 0)
            count = jnp.minimum(pos + 1, POOL_WINDOWS[j]).astype(jnp.float32)
            mean = w[HALO_ROWS:, :] / count
            d = (mean - v).astype(jnp.bfloat16)
            pooled = _dot(d, pw_ref[j])

            if j + 1 < n_groups:
                lo_next, hi_next = project(j + 1)
            else:
                h_ref[...] = _rmsnorm_f32(xn_ref[...], g_ref[...]).astype(jnp.bfloat16)
                plo_ref[...], phi_ref[...] = project(0)

            u = lo[:, c:2 * c] * hi[:, 0:c]
            b = hi[:, c:2 * c]
            u_prev = jnp.where(seq_start, 0.0, uhalo_ref[j])
            uhalo_ref[j] = u[tm - HALO_ROWS:, :]
            ue = jnp.concatenate([u_prev, u], axis=0)
            cw = cw_ref[:, c * j:c * (j + 1)]
            conv = (cw[0:1, :] * pltpu.roll(ue, 2, 0) + cw[1:2, :] * pltpu.roll(ue, 1, 0)
                    + cw[2:3, :] * ue)
            y_conv = b * conv[HALO_ROWS:, :]
            y_pool = pooled * ps_ref[:, c * j:c * (j + 1)]

            y = jnp.concatenate([y_conv, y_pool], axis=1).astype(jnp.bfloat16)
            out_j = _dot(y, wo_ref[2 * c * j:2 * c * (j + 1), :])
            if j == 0:
                o_ref[...] = res_ref[...] + out_j
                res_ref[...] = xn_ref[...]
            else:
                o_ref[...] += out_j
            if j + 1 < n_groups:
                lo, hi = lo_next, hi_next


def _mlp_kernel(x_ref, g_ref, wu_ref, wd_ref, gf_ref, *rest, final_norm, cast_next):
    if cast_next:
        wu_src_ref, wd_src_ref, o_ref, wu_dst_ref, wd_dst_ref, h_ref = rest
        wu_dst_ref[...] = wu_src_ref[...].astype(jnp.bfloat16)
        wd_dst_ref[...] = wd_src_ref[...].astype(jnp.bfloat16)
    else:
        o_ref, h_ref = rest
    k = pl.program_id(1)

    @pl.when(k == 0)
    def _():
        x = x_ref[...]
        h_ref[...] = _rmsnorm_f32(x, g_ref[...]).astype(jnp.bfloat16)
        o_ref[...] = x

    t = jnp.maximum(_dot(h_ref[...], wu_ref[...]), 0.0)
    o_ref[...] += _dot((t * t).astype(jnp.bfloat16), wd_ref[...])

    if final_norm:
        @pl.when(k == pl.num_programs(1) - 1)
        def _():
            o_ref[...] = _rmsnorm_f32(o_ref[...], gf_ref[...])


def _resident(block_shape, index_map):
    return pl.BlockSpec(block_shape, index_map, pipeline_mode=pl.Buffered(1))


def _mixer_layer(x2, g, win, cw, pw, ps, wo, *, layer, seq_len, token_tile):
    n_tok, d_model = x2.shape
    _, n_groups, c, _ = pw.shape
    d_conv = n_groups * c
    assert seq_len % token_tile == 0 and token_tile >= HALO_ROWS
    n_tiles = n_tok // token_tile
    kernel = functools.partial(_mixer_kernel, tiles_per_seq=seq_len // token_tile)
    return pl.pallas_call(
        kernel,
        grid=(n_tiles + 1,),
        in_specs=[
            pl.BlockSpec((token_tile, d_model), lambda s: (jnp.minimum(s, n_tiles - 1), 0)),
            _resident((None, 1, d_model), lambda s: (layer, 0, 0)),
            _resident((None, d_model, 4 * d_conv), lambda s: (layer, 0, 0)),
            _resident((None, CONV_WIDTH, d_conv), lambda s: (layer, 0, 0)),
            _resident((None, n_groups, c, c), lambda s: (layer, 0, 0, 0)),
            _resident((None, 1, d_conv), lambda s: (layer, 0, 0)),
            _resident((None, 2 * d_conv, d_model), lambda s: (layer, 0, 0)),
        ],
        out_specs=pl.BlockSpec((token_tile, d_model), lambda s: (jnp.maximum(s - 1, 0), 0)),
        out_shape=jax.ShapeDtypeStruct((n_tok, d_model), jnp.float32),
        scratch_shapes=[
            pltpu.VMEM((token_tile, d_model), jnp.bfloat16),
            pltpu.VMEM((token_tile, d_model), jnp.float32),
            pltpu.VMEM((token_tile, 2 * c), jnp.float32),
            pltpu.VMEM((token_tile, 2 * c), jnp.float32),
            pltpu.VMEM((n_groups, HALO_ROWS, c), jnp.float32),
            pltpu.VMEM((n_groups, HALO_ROWS, c), jnp.float32),
        ],
        compiler_params=pltpu.CompilerParams(
            dimension_semantics=("arbitrary",),
            vmem_limit_bytes=VMEM_LIMIT_BYTES),
        name="mixer",
    )(x2, g, win, cw, pw, ps, wo)


def _mlp_layer(x2, g, wu, wd, gf, w_up, w_down, *, layer, token_tile, ff_chunk, final_norm):
    n_tok, d_model = x2.shape
    d_ff = wu.shape[-1]
    depth = w_up.shape[0]
    n_tiles, nk = n_tok // token_tile, d_ff // ff_chunk
    cast_next = layer + 1 < depth
    kernel = functools.partial(_mlp_kernel, final_norm=final_norm, cast_next=cast_next)
    in_specs = [
        pl.BlockSpec((token_tile, d_model), lambda i, k: (i, 0)),
        pl.BlockSpec((None, 1, d_model), lambda i, k: (layer, 0, 0)),
        pl.BlockSpec((d_model, ff_chunk), lambda i, k: (0, k)),
        pl.BlockSpec((ff_chunk, d_model), lambda i, k: (k, 0)),
        pl.BlockSpec((1, d_model), lambda i, k: (0, 0)),
    ]
    out_specs = [pl.BlockSpec((token_tile, d_model), lambda i, k: (i, 0))]
    out_shape = [jax.ShapeDtypeStruct((n_tok, d_model), jnp.float32)]
    operands = [x2, g, wu, wd, gf]
    if cast_next:
        n_steps = n_tiles * nk
        for w in (w_up, w_down):
            rows, cols = w.shape[1:]
            blk = max(BF16_SUBLANES, rows // n_steps)
            assert rows % blk == 0 and n_steps % (rows // blk) == 0
            reps = n_steps // (rows // blk)
            in_specs.append(pl.BlockSpec(
                (None, blk, cols), lambda i, k, reps=reps: (layer + 1, (i * nk + k) // reps, 0)))
            out_specs.append(pl.BlockSpec(
                (blk, cols), lambda i, k, reps=reps: ((i * nk + k) // reps, 0)))
            out_shape.append(jax.ShapeDtypeStruct((rows, cols), jnp.bfloat16))
            operands.append(w)
    outs = pl.pallas_call(
        kernel,
        grid=(n_tiles, nk),
        in_specs=in_specs,
        out_specs=out_specs,
        out_shape=out_shape,
        scratch_shapes=[pltpu.VMEM((token_tile, d_model), jnp.bfloat16)],
        compiler_params=pltpu.CompilerParams(
            dimension_semantics=("arbitrary", "arbitrary"),
            vmem_limit_bytes=VMEM_LIMIT_BYTES),
        name="mlp_final" if final_norm else "mlp",
    )(*operands)
    if cast_next:
        return outs
    return outs[0], None, None


def _relayout_kernel(w_ref, o_ref):
    o_ref[...] = w_ref[...].astype(o_ref.dtype)


def _chunk_major_in(w_in, n_groups, c):
    depth, d_model, width = w_in.shape
    n_types = width // (n_groups * c)
    return pl.pallas_call(
        _relayout_kernel,
        grid=(depth, n_types, n_groups),
        in_specs=[pl.BlockSpec((None, d_model, c), lambda l, t, j: (l, 0, t * n_groups + j))],
        out_specs=pl.BlockSpec((None, d_model, c), lambda l, t, j: (l, 0, j * n_types + n_types - 1 - t)),
        out_shape=jax.ShapeDtypeStruct(w_in.shape, jnp.bfloat16),
        name="relayout_in",
    )(w_in)


def _chunk_major_out(w_out, n_groups, c):
    depth, rows, d_model = w_out.shape
    n_types = rows // (n_groups * c)
    return pl.pallas_call(
        _relayout_kernel,
        grid=(depth, n_types, n_groups),
        in_specs=[pl.BlockSpec((None, c, d_model), lambda l, t, j: (l, t * n_groups + j, 0))],
        out_specs=pl.BlockSpec((None, c, d_model), lambda l, t, j: (l, j * n_types + t, 0)),
        out_shape=jax.ShapeDtypeStruct(w_out.shape, jnp.bfloat16),
        name="relayout_out",
    )(w_out)


def kernel(x, w_in, conv_w, pool_w, pool_scale, w_out, norm_mix, norm_mlp, w_up, w_down, norm_final):
    batch, seq_len, d_model = x.shape
    depth = w_in.shape[0]
    n_groups, group_dim = pool_w.shape[1], pool_w.shape[2]
    d_conv = conv_w.shape[-1]
    assert n_groups == len(POOL_WINDOWS) and all(
        w == 2 ** (g + 1) for g, w in enumerate(POOL_WINDOWS))
    assert d_conv == n_groups * group_dim and w_in.shape[-1] == 4 * d_conv
    assert conv_w.shape[1] == CONV_WIDTH

    win = _chunk_major_in(w_in, n_groups, group_dim)
    wo = _chunk_major_out(w_out, n_groups, group_dim)
    pw = pool_w.astype(jnp.bfloat16)
    wu = w_up[0].astype(jnp.bfloat16)
    wd = w_down[0].astype(jnp.bfloat16)
    g_mix = norm_mix.reshape(depth, 1, d_model)
    g_mlp = norm_mlp.reshape(depth, 1, d_model)
    ps = pool_scale.reshape(depth, 1, d_conv)
    gf = norm_final.reshape(1, d_model)

    xs = x.reshape(batch * seq_len, d_model)
    for l in range(depth):
        xs = _mixer_layer(xs, g_mix, win, conv_w, pw, ps, wo, layer=l, seq_len=seq_len,
                          token_tile=min(MIXER_TOKEN_TILE, seq_len))
        xs, wu, wd = _mlp_layer(xs, g_mlp, wu, wd, gf, w_up, w_down, layer=l,
                                token_tile=min(MLP_TOKEN_TILE, batch * seq_len),
                                ff_chunk=min(MLP_FF_CHUNK, w_up.shape[-1]),
                                final_norm=(l == depth - 1))
    return xs.reshape(batch, seq_len, d_model)
```

```python
import functools

import jax
import jax.numpy as jnp
from jax import lax
from jax.experimental import pallas as pl
from jax.experimental.pallas import tpu as pltpu

EPS = 1e-6
CONV_WIDTH = 3
POOL_WINDOWS = (2, 4, 8, 16)
HALO_ROWS = 16
BF16_SUBLANES = 16

VMEM_LIMIT_BYTES = 60 * 1024 * 1024

MIXER_TOKEN_TILE = 512
MLP_TOKEN_TILE = 1024
MLP_FF_CHUNK = 1024


def _rmsnorm_f32(x, g):
    r = lax.rsqrt(jnp.mean(x * x, axis=-1, keepdims=True) + EPS)
    return (x * r) * g


def _dot(a, b):
    return jnp.dot(a, b, preferred_element_type=jnp.float32)


def _mixer_kernel(xn_ref, g_ref, win_ref, cw_ref, pw_ref, ps_ref, wo_ref, o_ref,
                  h_ref, res_ref, plo_ref, phi_ref, uhalo_ref, vhalo_ref, *, tiles_per_seq):
    s = pl.program_id(0)
    tm = xn_ref.shape[0]
    n_groups, _, c = pw_ref.shape

    def project(j):
        lo = _dot(h_ref[...], win_ref[:, 4 * c * j:4 * c * j + 2 * c])
        hi = _dot(h_ref[...], win_ref[:, 4 * c * j + 2 * c:4 * c * (j + 1)])
        return lo, hi

    @pl.when(s == 0)
    def _():
        xn = xn_ref[...]
        h_ref[...] = _rmsnorm_f32(xn, g_ref[...]).astype(jnp.bfloat16)
        res_ref[...] = xn
        plo_ref[...], phi_ref[...] = project(0)

    @pl.when(s > 0)
    def _():
        tile_in_seq = (s - 1) % tiles_per_seq
        seq_start = tile_in_seq == 0
        pos = tile_in_seq * tm + lax.broadcasted_iota(jnp.int32, (tm, 1), 0)

        lo, hi = plo_ref[...], phi_ref[...]
        for j in range(n_groups):
            v = lo[:, 0:c]
            v_prev = jnp.where(seq_start, 0.0, vhalo_ref[j])
            vhalo_ref[j] = v[tm - HALO_ROWS:, :]
            w = jnp.concatenate([v_prev, v], axis=0)
            for k in range(j + 1):
                w = w + pltpu.roll(w, 2 ** k, 0)
            count = jnp.minimum(pos + 1, POOL_WINDOWS[j]).astype(jnp.float32)
            mean = w[HALO_ROWS:, :] / count
            d = (mean - v).astype(jnp.bfloat16)
            pooled = _dot(d, pw_ref[j])

            if j + 1 < n_groups:
                lo_next, hi_next = project(j + 1)
            else:
                h_ref[...] = _rmsnorm_f32(xn_ref[...], g_ref[...]).astype(jnp.bfloat16)
                plo_ref[...], phi_ref[...] = project(0)

            u = lo[:, c:2 * c] * hi[:, 0:c]
            b = hi[:, c:2 * c]
            u_prev = jnp.where(seq_start, 0.0, uhalo_ref[j])
            uhalo_ref[j] = u[tm - HALO_ROWS:, :]
            ue = jnp.concatenate([u_prev, u], axis=0)
            cw = cw_ref[:, c * j:c * (j + 1)]
            conv = (cw[0:1, :] * pltpu.roll(ue, 2, 0) + cw[1:2, :] * pltpu.roll(ue, 1, 0)
                    + cw[2:3, :] * ue)
            y_conv = b * conv[HALO_ROWS:, :]
            y_pool = pooled * ps_ref[:, c * j:c * (j + 1)]

            y = jnp.concatenate([y_conv, y_pool], axis=1).astype(jnp.bfloat16)
            out_j = _dot(y, wo_ref[2 * c * j:2 * c * (j + 1), :])
            if j == 0:
                o_ref[...] = res_ref[...] + out_j
                res_ref[...] = xn_ref[...]
            else:
                o_ref[...] += out_j
            if j + 1 < n_groups:
                lo, hi = lo_next, hi_next


def _mlp_kernel(x_ref, g_ref, wu_ref, wd_ref, gf_ref, *rest, final_norm, cast_next):
    if cast_next:
        wu_src_ref, wd_src_ref, o_ref, wu_dst_ref, wd_dst_ref, h_ref = rest
        wu_dst_ref[...] = wu_src_ref[...].astype(jnp.bfloat16)
        wd_dst_ref[...] = wd_src_ref[...].astype(jnp.bfloat16)
    else:
        o_ref, h_ref = rest
    k = pl.program_id(1)

    @pl.when(k == 0)
    def _():
        x = x_ref[...]
        h_ref[...] = _rmsnorm_f32(x, g_ref[...]).astype(jnp.bfloat16)
        o_ref[...] = x

    t = jnp.maximum(_dot(h_ref[...], wu_ref[...]), 0.0)
    o_ref[...] += _dot((t * t).astype(jnp.bfloat16), wd_ref[...])

    if final_norm:
        @pl.when(k == pl.num_programs(1) - 1)
        def _():
            o_ref[...] = _rmsnorm_f32(o_ref[...], gf_ref[...])


def _resident(block_shape, index_map):
    return pl.BlockSpec(block_shape, index_map, pipeline_mode=pl.Buffered(1))


def _mixer_layer(x2, g, win, cw, pw, ps, wo, *, layer, seq_len, token_tile):
    n_tok, d_model = x2.shape
    _, n_groups, c, _ = pw.shape
    d_conv = n_groups * c
    assert seq_len % token_tile == 0 and token_tile >= HALO_ROWS
    n_tiles = n_tok // token_tile
    kernel = functools.partial(_mixer_kernel, tiles_per_seq=seq_len // token_tile)
    return pl.pallas_call(
        kernel,
        grid=(n_tiles + 1,),
        in_specs=[
            pl.BlockSpec((token_tile, d_model), lambda s: (jnp.minimum(s, n_tiles - 1), 0)),
            _resident((None, 1, d_model), lambda s: (layer, 0, 0)),
            _resident((None, d_model, 4 * d_conv), lambda s: (layer, 0, 0)),
            _resident((None, CONV_WIDTH, d_conv), lambda s: (layer, 0, 0)),
            _resident((None, n_groups, c, c), lambda s: (layer, 0, 0, 0)),
            _resident((None, 1, d_conv), lambda s: (layer, 0, 0)),
            _resident((None, 2 * d_conv, d_model), lambda s: (layer, 0, 0)),
        ],
        out_specs=pl.BlockSpec((token_tile, d_model), lambda s: (jnp.maximum(s - 1, 0), 0)),
        out_shape=jax.ShapeDtypeStruct((n_tok, d_model), jnp.float32),
        scratch_shapes=[
            pltpu.VMEM((token_tile, d_model), jnp.bfloat16),
            pltpu.VMEM((token_tile, d_model), jnp.float32),
            pltpu.VMEM((token_tile, 2 * c), jnp.float32),
            pltpu.VMEM((token_tile, 2 * c), jnp.float32),
            pltpu.VMEM((n_groups, HALO_ROWS, c), jnp.float32),
            pltpu.VMEM((n_groups, HALO_ROWS, c), jnp.float32),
        ],
        compiler_params=pltpu.CompilerParams(
            dimension_semantics=("arbitrary",),
            vmem_limit_bytes=VMEM_LIMIT_BYTES),
        name="mixer",
    )(x2, g, win, cw, pw, ps, wo)


def _mlp_layer(x2, g, wu, wd, gf, w_up, w_down, *, layer, token_tile, ff_chunk, final_norm):
    n_tok, d_model = x2.shape
    d_ff = wu.shape[-1]
    depth = w_up.shape[0]
    n_tiles, nk = n_tok // token_tile, d_ff // ff_chunk
    cast_next = layer + 1 < depth
    kernel = functools.partial(_mlp_kernel, final_norm=final_norm, cast_next=cast_next)
    in_specs = [
        pl.BlockSpec((token_tile, d_model), lambda i, k: (i, 0)),
        pl.BlockSpec((None, 1, d_model), lambda i, k: (layer, 0, 0)),
        pl.BlockSpec((d_model, ff_chunk), lambda i, k: (0, k)),
        pl.BlockSpec((ff_chunk, d_model), lambda i, k: (k, 0)),
        pl.BlockSpec((1, d_model), lambda i, k: (0, 0)),
    ]
    out_specs = [pl.BlockSpec((token_tile, d_model), lambda i, k: (i, 0))]
    out_shape = [jax.ShapeDtypeStruct((n_tok, d_model), jnp.float32)]
    operands = [x2, g, wu, wd, gf]
    if cast_next:
        n_steps = n_tiles * nk
        for w in (w_up, w_down):
            rows, cols = w.shape[1:]
            blk = max(BF16_SUBLANES, rows // n_steps)
            assert rows % blk == 0 and n_steps % (rows // blk) == 0
            reps = n_steps // (rows // blk)
            in_specs.append(pl.BlockSpec(
                (None, blk, cols), lambda i, k, reps=reps: (layer + 1, (i * nk + k) // reps, 0)))
            out_specs.append(pl.BlockSpec(
                (blk, cols), lambda i, k, reps=reps: ((i * nk + k) // reps, 0)))
            out_shape.append(jax.ShapeDtypeStruct((rows, cols), jnp.bfloat16))
            operands.append(w)
    outs = pl.pallas_call(
        kernel,
        grid=(n_tiles, nk),
        in_specs=in_specs,
        out_specs=out_specs,
        out_shape=out_shape,
        scratch_shapes=[pltpu.VMEM((token_tile, d_model), jnp.bfloat16)],
        compiler_params=pltpu.CompilerParams(
            dimension_semantics=("arbitrary", "arbitrary"),
            vmem_limit_bytes=VMEM_LIMIT_BYTES),
        name="mlp_final" if final_norm else "mlp",
    )(*operands)
    if cast_next:
        return outs
    return outs[0], None, None


def _relayout_kernel(w_ref, o_ref):
    o_ref[...] = w_ref[...].astype(o_ref.dtype)


def _chunk_major_in(w_in, n_groups, c):
    depth, d_model, width = w_in.shape
    n_types = width // (n_groups * c)
    return pl.pallas_call(
        _relayout_kernel,
        grid=(depth, n_types, n_groups),
        in_specs=[pl.BlockSpec((None, d_model, c), lambda l, t, j: (l, 0, t * n_groups + j))],
        out_specs=pl.BlockSpec((None, d_model, c), lambda l, t, j: (l, 0, j * n_types + n_types - 1 - t)),
        out_shape=jax.ShapeDtypeStruct(w_in.shape, jnp.bfloat16),
        name="relayout_in",
    )(w_in)


def _chunk_major_out(w_out, n_groups, c):
    depth, rows, d_model = w_out.shape
    n_types = rows // (n_groups * c)
    return pl.pallas_call(
        _relayout_kernel,
        grid=(depth, n_types, n_groups),
        in_specs=[pl.BlockSpec((None, c, d_model), lambda l, t, j: (l, t * n_groups + j, 0))],
        out_specs=pl.BlockSpec((None, c, d_model), lambda l, t, j: (l, j * n_types + t, 0)),
        out_shape=jax.ShapeDtypeStruct(w_out.shape, jnp.bfloat16),
        name="relayout_out",
    )(w_out)


def kernel(x, w_in, conv_w, pool_w, pool_scale, w_out, norm_mix, norm_mlp, w_up, w_down, norm_final):
    batch, seq_len, d_model = x.shape
    depth = w_in.shape[0]
    n_groups, group_dim = pool_w.shape[1], pool_w.shape[2]
    d_conv = conv_w.shape[-1]
    assert n_groups == len(POOL_WINDOWS) and all(
        w == 2 ** (g + 1) for g, w in enumerate(POOL_WINDOWS))
    assert d_conv == n_groups * group_dim and w_in.shape[-1] == 4 * d_conv
    assert conv_w.shape[1] == CONV_WIDTH

    win = _chunk_major_in(w_in, n_groups, group_dim)
    wo = _chunk_major_out(w_out, n_groups, group_dim)
    pw = pool_w.astype(jnp.bfloat16)
    wu = w_up[0].astype(jnp.bfloat16)
    wd = w_down[0].astype(jnp.bfloat16)
    g_mix = norm_mix.reshape(depth, 1, d_model)
    g_mlp = norm_mlp.reshape(depth, 1, d_model)
    ps = pool_scale.reshape(depth, 1, d_conv)
    gf = norm_final.reshape(1, d_model)

    xs = x.reshape(batch * seq_len, d_model)
    for l in range(depth):
        xs = _mixer_layer(xs, g_mix, win, conv_w, pw, ps, wo, layer=l, seq_len=seq_len,
                          token_tile=min(MIXER_TOKEN_TILE, seq_len))
        xs, wu, wd = _mlp_layer(xs, g_mlp, wu, wd, gf, w_up, w_down, layer=l,
                                token_tile=min(MLP_TOKEN_TILE, batch * seq_len),
                                ff_chunk=min(MLP_FF_CHUNK, w_up.shape[-1]),
                                final_norm=(l == depth - 1))
    return xs.reshape(batch, seq_len, d_model)
```

```python
import functools

import jax
import jax.numpy as jnp
from jax import lax
from jax.experimental import pallas as pl
from jax.experimental.pallas import tpu as pltpu

EPS = 1e-6
CONV_WIDTH = 3
POOL_WINDOWS = (2, 4, 8, 16)
HALO_ROWS = 16
BF16_SUBLANES = 16

VMEM_LIMIT_BYTES = 60 * 1024 * 1024

MIXER_TOKEN_TILE = 512
MLP_TOKEN_TILE = 512
MLP_FF_CHUNK = 2048


def _rmsnorm_f32(x, g):
    r = lax.rsqrt(jnp.mean(x * x, axis=-1, keepdims=True) + EPS)
    return (x * r) * g


def _dot(a, b):
    return jnp.dot(a, b, preferred_element_type=jnp.float32)


def _mixer_kernel(xn_ref, g_ref, win_ref, cw_ref, pw_ref, ps_ref, wo_ref, o_ref,
                  h_ref, res_ref, plo_ref, phi_ref, uhalo_ref, vhalo_ref, *, tiles_per_seq):
    s = pl.program_id(0)
    tm = xn_ref.shape[0]
    n_groups, _, c = pw_ref.shape

    def project(j):
        lo = _dot(h_ref[...], win_ref[:, 4 * c * j:4 * c * j + 2 * c])
        hi = _dot(h_ref[...], win_ref[:, 4 * c * j + 2 * c:4 * c * (j + 1)])
        return lo, hi

    @pl.when(s == 0)
    def _():
        xn = xn_ref[...]
        h_ref[...] = _rmsnorm_f32(xn, g_ref[...]).astype(jnp.bfloat16)
        res_ref[...] = xn
        plo_ref[...], phi_ref[...] = project(0)

    @pl.when(s > 0)
    def _():
        tile_in_seq = (s - 1) % tiles_per_seq
        seq_start = tile_in_seq == 0
        pos = tile_in_seq * tm + lax.broadcasted_iota(jnp.int32, (tm, 1), 0)

        lo, hi = plo_ref[...], phi_ref[...]
        for j in range(n_groups):
            v = lo[:, 0:c]
            v_prev = jnp.where(seq_start, 0.0, vhalo_ref[j])
            vhalo_ref[j] = v[tm - HALO_ROWS:, :]
            w = jnp.concatenate([v_prev, v], axis=0)
            for k in range(j + 1):
                w = w + pltpu.roll(w, 2 ** k, 0)
            count = jnp.minimum(pos + 1, POOL_WINDOWS[j]).astype(jnp.float32)
            mean = w[HALO_ROWS:, :] / count
            d = (mean - v).astype(jnp.bfloat16)
            pooled = _dot(d, pw_ref[j])

            if j + 1 < n_groups:
                lo_next, hi_next = project(j + 1)
            else:
                h_ref[...] = _rmsnorm_f32(xn_ref[...], g_ref[...]).astype(jnp.bfloat16)
                plo_ref[...], phi_ref[...] = project(0)

            u = lo[:, c:2 * c] * hi[:, 0:c]
            b = hi[:, c:2 * c]
            u_prev = jnp.where(seq_start, 0.0, uhalo_ref[j])
            uhalo_ref[j] = u[tm - HALO_ROWS:, :]
            ue = jnp.concatenate([u_prev, u], axis=0)
            cw = cw_ref[:, c * j:c * (j + 1)]
            conv = (cw[0:1, :] * pltpu.roll(ue, 2, 0) + cw[1:2, :] * pltpu.roll(ue, 1, 0)
                    + cw[2:3, :] * ue)
            y_conv = b * conv[HALO_ROWS:, :]
            y_pool = pooled * ps_ref[:, c * j:c * (j + 1)]

            y = jnp.concatenate([y_conv, y_pool], axis=1).astype(jnp.bfloat16)
            out_j = _dot(y, wo_ref[2 * c * j:2 * c * (j + 1), :])
            if j == 0:
                o_ref[...] = res_ref[...] + out_j
                res_ref[...] = xn_ref[...]
            else:
                o_ref[...] += out_j
            if j + 1 < n_groups:
                lo, hi = lo_next, hi_next


def _mlp_kernel(x_ref, g_ref, wu_ref, wd_ref, gf_ref, *rest, final_norm, cast_next):
    if cast_next:
        wu_src_ref, wd_src_ref, o_ref, wu_dst_ref, wd_dst_ref, h_ref = rest
        wu_dst_ref[...] = wu_src_ref[...].astype(jnp.bfloat16)
        wd_dst_ref[...] = wd_src_ref[...].astype(jnp.bfloat16)
    else:
        o_ref, h_ref = rest
    k = pl.program_id(1)

    def down(first):
        t = jnp.maximum(_dot(h_ref[...], wu_ref[...]), 0.0)
        y = _dot((t * t).astype(jnp.bfloat16), wd_ref[...])
        o_ref[...] = (x_ref[...] if first else o_ref[...]) + y

    @pl.when(k == 0)
    def _():
        h_ref[...] = _rmsnorm_f32(x_ref[...], g_ref[...]).astype(jnp.bfloat16)
        down(first=True)

    @pl.when(k > 0)
    def _():
        down(first=False)

    if final_norm:
        @pl.when(k == pl.num_programs(1) - 1)
        def _():
            o_ref[...] = _rmsnorm_f32(o_ref[...], gf_ref[...])


def _resident(block_shape, index_map):
    return pl.BlockSpec(block_shape, index_map, pipeline_mode=pl.Buffered(1))


def _mixer_layer(x2, g, win, cw, pw, ps, wo, *, layer, seq_len, token_tile):
    n_tok, d_model = x2.shape
    _, n_groups, c, _ = pw.shape
    d_conv = n_groups * c
    assert seq_len % token_tile == 0 and token_tile >= HALO_ROWS
    n_tiles = n_tok // token_tile
    kernel = functools.partial(_mixer_kernel, tiles_per_seq=seq_len // token_tile)
    return pl.pallas_call(
        kernel,
        grid=(n_tiles + 1,),
        in_specs=[
            pl.BlockSpec((token_tile, d_model), lambda s: (jnp.minimum(s, n_tiles - 1), 0)),
            _resident((None, 1, d_model), lambda s: (layer, 0, 0)),
            _resident((None, d_model, 4 * d_conv), lambda s: (layer, 0, 0)),
            _resident((None, CONV_WIDTH, d_conv), lambda s: (layer, 0, 0)),
            _resident((None, n_groups, c, c), lambda s: (layer, 0, 0, 0)),
            _resident((None, 1, d_conv), lambda s: (layer, 0, 0)),
            _resident((None, 2 * d_conv, d_model), lambda s: (layer, 0, 0)),
        ],
        out_specs=pl.BlockSpec((token_tile, d_model), lambda s: (jnp.maximum(s - 1, 0), 0)),
        out_shape=jax.ShapeDtypeStruct((n_tok, d_model), jnp.float32),
        scratch_shapes=[
            pltpu.VMEM((token_tile, d_model), jnp.bfloat16),
            pltpu.VMEM((token_tile, d_model), jnp.float32),
            pltpu.VMEM((token_tile, 2 * c), jnp.float32),
            pltpu.VMEM((token_tile, 2 * c), jnp.float32),
            pltpu.VMEM((n_groups, HALO_ROWS, c), jnp.float32),
            pltpu.VMEM((n_groups, HALO_ROWS, c), jnp.float32),
        ],
        compiler_params=pltpu.CompilerParams(
            dimension_semantics=("arbitrary",),
            vmem_limit_bytes=VMEM_LIMIT_BYTES),
        name="mixer",
    )(x2, g, win, cw, pw, ps, wo)


def _mlp_layer(x2, g, wu, wd, gf, w_up, w_down, *, layer, token_tile, ff_chunk, final_norm):
    n_tok, d_model = x2.shape
    d_ff = wu.shape[-1]
    depth = w_up.shape[0]
    n_tiles, nk = n_tok // token_tile, d_ff // ff_chunk
    cast_next = layer + 1 < depth
    kernel = functools.partial(_mlp_kernel, final_norm=final_norm, cast_next=cast_next)
    in_specs = [
        pl.BlockSpec((token_tile, d_model), lambda i, k: (i, 0)),
        pl.BlockSpec((None, 1, d_model), lambda i, k: (layer, 0, 0)),
        pl.BlockSpec((d_model, ff_chunk), lambda i, k: (0, k)),
        pl.BlockSpec((ff_chunk, d_model), lambda i, k: (k, 0)),
        pl.BlockSpec((1, d_model), lambda i, k: (0, 0)),
    ]
    out_specs = [pl.BlockSpec((token_tile, d_model), lambda i, k: (i, 0))]
    out_shape = [jax.ShapeDtypeStruct((n_tok, d_model), jnp.float32)]
    operands = [x2, g, wu, wd, gf]
    if cast_next:
        n_steps = n_tiles * nk
        for w in (w_up, w_down):
            rows, cols = w.shape[1:]
            blk = max(BF16_SUBLANES, rows // n_steps)
            assert rows % blk == 0 and n_steps % (rows // blk) == 0
            reps = n_steps // (rows // blk)
            in_specs.append(pl.BlockSpec(
                (None, blk, cols), lambda i, k, reps=reps: (layer + 1, (i * nk + k) // reps, 0)))
            out_specs.append(pl.BlockSpec(
                (blk, cols), lambda i, k, reps=reps: ((i * nk + k) // reps, 0)))
            out_shape.append(jax.ShapeDtypeStruct((rows, cols), jnp.bfloat16))
            operands.append(w)
    outs = pl.pallas_call(
        kernel,
        grid=(n_tiles, nk),
        in_specs=in_specs,
        out_specs=out_specs,
        out_shape=out_shape,
        scratch_shapes=[pltpu.VMEM((token_tile, d_model), jnp.bfloat16)],
        compiler_params=pltpu.CompilerParams(
            dimension_semantics=("arbitrary", "arbitrary"),
            vmem_limit_bytes=VMEM_LIMIT_BYTES),
        name="mlp_final" if final_norm else "mlp",
    )(*operands)
    if cast_next:
        return outs
    return outs[0], None, None


def _relayout_kernel(w_ref, o_ref):
    o_ref[...] = w_ref[...].astype(o_ref.dtype)


def _chunk_major_in(w_in, n_groups, c):
    depth, d_model, width = w_in.shape
    n_types = width // (n_groups * c)
    return pl.pallas_call(
        _relayout_kernel,
        grid=(depth, n_types, n_groups),
        in_specs=[pl.BlockSpec((None, d_model, c), lambda l, t, j: (l, 0, t * n_groups + j))],
        out_specs=pl.BlockSpec((None, d_model, c), lambda l, t, j: (l, 0, j * n_types + n_types - 1 - t)),
        out_shape=jax.ShapeDtypeStruct(w_in.shape, jnp.bfloat16),
        name="relayout_in",
    )(w_in)


def _chunk_major_out(w_out, n_groups, c):
    depth, rows, d_model = w_out.shape
    n_types = rows // (n_groups * c)
    return pl.pallas_call(
        _relayout_kernel,
        grid=(depth, n_types, n_groups),
        in_specs=[pl.BlockSpec((None, c, d_model), lambda l, t, j: (l, t * n_groups + j, 0))],
        out_specs=pl.BlockSpec((None, c, d_model), lambda l, t, j: (l, j * n_types + t, 0)),
        out_shape=jax.ShapeDtypeStruct(w_out.shape, jnp.bfloat16),
        name="relayout_out",
    )(w_out)


def kernel(x, w_in, conv_w, pool_w, pool_scale, w_out, norm_mix, norm_mlp, w_up, w_down, norm_final):
    batch, seq_len, d_model = x.shape
    depth = w_in.shape[0]
    n_groups, group_dim = pool_w.shape[1], pool_w.shape[2]
    d_conv = conv_w.shape[-1]
    assert n_groups == len(POOL_WINDOWS) and all(
        w == 2 ** (g + 1) for g, w in enumerate(POOL_WINDOWS))
    assert d_conv == n_groups * group_dim and w_in.shape[-1] == 4 * d_conv
    assert conv_w.shape[1] == CONV_WIDTH

    win = _chunk_major_in(w_in, n_groups, group_dim)
    wo = _chunk_major_out(w_out, n_groups, group_dim)
    pw = pool_w.astype(jnp.bfloat16)
    wu = w_up[0].astype(jnp.bfloat16)
    wd = w_down[0].astype(jnp.bfloat16)
    g_mix = norm_mix.reshape(depth, 1, d_model)
    g_mlp = norm_mlp.reshape(depth, 1, d_model)
    ps = pool_scale.reshape(depth, 1, d_conv)
    gf = norm_final.reshape(1, d_model)

    xs = x.reshape(batch * seq_len, d_model)
    for l in range(depth):
        xs = _mixer_layer(xs, g_mix, win, conv_w, pw, ps, wo, layer=l, seq_len=seq_len,
                          token_tile=min(MIXER_TOKEN_TILE, seq_len))
        xs, wu, wd = _mlp_layer(xs, g_mlp, wu, wd, gf, w_up, w_down, layer=l,
                                token_tile=min(MLP_TOKEN_TILE, batch * seq_len),
                                ff_chunk=min(MLP_FF_CHUNK, w_up.shape[-1]),
                                final_norm=(l == depth - 1))
    return xs.reshape(batch, seq_len, d_model)
```

```python
import functools

import jax
import jax.numpy as jnp
from jax import lax
from jax.experimental import pallas as pl
from jax.experimental.pallas import tpu as pltpu

EPS = 1e-6
CONV_WIDTH = 3
POOL_WINDOWS = (2, 4, 8, 16)
HALO_ROWS = 16
BF16_SUBLANES = 16

VMEM_LIMIT_BYTES = 60 * 1024 * 1024

MIXER_TOKEN_TILE = 512
MLP_TOKEN_TILE = 512
MLP_FF_CHUNK = 2048


def _rmsnorm_f32(x, g):
    r = lax.rsqrt(jnp.mean(x * x, axis=-1, keepdims=True) + EPS)
    return (x * r) * g


def _dot(a, b):
    return jnp.dot(a, b, preferred_element_type=jnp.float32)


def _mixer_kernel(xn_ref, g_ref, win_ref, cw_ref, pw_ref, ps_ref, wo_ref, *rest,
                  tiles_per_seq, relayout_next):
    if relayout_next:
        win_src_ref, wo_src_ref, o_ref, win_dst_ref, wo_dst_ref = rest[:5]
        win_dst_ref[...] = win_src_ref[...].astype(jnp.bfloat16)
        wo_dst_ref[...] = wo_src_ref[...].astype(jnp.bfloat16)
        rest = rest[5:]
    else:
        o_ref, rest = rest[0], rest[1:]
    h_ref, res_ref, plo_ref, phi_ref, uhalo_ref, vhalo_ref = rest
    s = pl.program_id(0)
    tm = xn_ref.shape[0]
    n_groups, _, c = pw_ref.shape

    def project(j):
        lo = _dot(h_ref[...], win_ref[:, 4 * c * j:4 * c * j + 2 * c])
        hi = _dot(h_ref[...], win_ref[:, 4 * c * j + 2 * c:4 * c * (j + 1)])
        return lo, hi

    @pl.when(s == 0)
    def _():
        xn = xn_ref[...]
        h_ref[...] = _rmsnorm_f32(xn, g_ref[...]).astype(jnp.bfloat16)
        res_ref[...] = xn
        plo_ref[...], phi_ref[...] = project(0)

    @pl.when(s > 0)
    def _():
        tile_in_seq = (s - 1) % tiles_per_seq
        seq_start = tile_in_seq == 0
        pos = tile_in_seq * tm + lax.broadcasted_iota(jnp.int32, (tm, 1), 0)

        lo, hi = plo_ref[...], phi_ref[...]
        for j in range(n_groups):
            v = lo[:, 0:c]
            v_prev = jnp.where(seq_start, 0.0, vhalo_ref[j])
            vhalo_ref[j] = v[tm - HALO_ROWS:, :]
            w = jnp.concatenate([v_prev, v], axis=0)
            for k in range(j + 1):
                w = w + pltpu.roll(w, 2 ** k, 0)
            count = jnp.minimum(pos + 1, POOL_WINDOWS[j]).astype(jnp.float32)
            mean = w[HALO_ROWS:, :] / count
            d = (mean - v).astype(jnp.bfloat16)
            pooled = _dot(d, pw_ref[j])

            if j + 1 < n_groups:
                lo_next, hi_next = project(j + 1)
            else:
                h_ref[...] = _rmsnorm_f32(xn_ref[...], g_ref[...]).astype(jnp.bfloat16)
                plo_ref[...], phi_ref[...] = project(0)

            u = lo[:, c:2 * c] * hi[:, 0:c]
            b = hi[:, c:2 * c]
            u_prev = jnp.where(seq_start, 0.0, uhalo_ref[j])
            uhalo_ref[j] = u[tm - HALO_ROWS:, :]
            ue = jnp.concatenate([u_prev, u], axis=0)
            cw = cw_ref[:, c * j:c * (j + 1)]
            conv = (cw[0:1, :] * pltpu.roll(ue, 2, 0) + cw[1:2, :] * pltpu.roll(ue, 1, 0)
                    + cw[2:3, :] * ue)
            y_conv = b * conv[HALO_ROWS:, :]
            y_pool = pooled * ps_ref[:, c * j:c * (j + 1)]

            y = jnp.concatenate([y_conv, y_pool], axis=1).astype(jnp.bfloat16)
            out_j = _dot(y, wo_ref[2 * c * j:2 * c * (j + 1), :])
            if j == 0:
                o_ref[...] = res_ref[...] + out_j
                res_ref[...] = xn_ref[...]
            else:
                o_ref[...] += out_j
            if j + 1 < n_groups:
                lo, hi = lo_next, hi_next


def _mlp_kernel(x_ref, g_ref, wu_ref, wd_ref, gf_ref, *rest, final_norm, cast_next):
    if cast_next:
        wu_src_ref, wd_src_ref, o_ref, wu_dst_ref, wd_dst_ref, h_ref = rest
        wu_dst_ref[...] = wu_src_ref[...].astype(jnp.bfloat16)
        wd_dst_ref[...] = wd_src_ref[...].astype(jnp.bfloat16)
    else:
        o_ref, h_ref = rest
    k = pl.program_id(1)

    def down(first):
        t = jnp.maximum(_dot(h_ref[...], wu_ref[...]), 0.0)
        y = _dot((t * t).astype(jnp.bfloat16), wd_ref[...])
        o_ref[...] = (x_ref[...] if first else o_ref[...]) + y

    @pl.when(k == 0)
    def _():
        h_ref[...] = _rmsnorm_f32(x_ref[...], g_ref[...]).astype(jnp.bfloat16)
        down(first=True)

    @pl.when(k > 0)
    def _():
        down(first=False)

    if final_norm:
        @pl.when(k == pl.num_programs(1) - 1)
        def _():
            o_ref[...] = _rmsnorm_f32(o_ref[...], gf_ref[...])


def _resident(block_shape, index_map):
    return pl.BlockSpec(block_shape, index_map, pipeline_mode=pl.Buffered(1))


def _chunk_major_in_blocks(w_in, layer, n_steps, n_groups, c):
    d_model, width = w_in.shape[1:]
    n_col = width // c
    n_types = n_col // n_groups
    assert n_steps % n_col == 0
    rb = d_model // (n_steps // n_col)
    assert rb % BF16_SUBLANES == 0

    def src(s):
        s = jnp.minimum(s, n_steps - 1)
        return (layer, s // n_col, s % n_col)

    def dst(s):
        s = jnp.minimum(s, n_steps - 1)
        q = s % n_col
        return (s // n_col, (q % n_groups) * n_types + n_types - 1 - q // n_groups)

    return (pl.BlockSpec((None, rb, c), src), pl.BlockSpec((rb, c), dst),
            jax.ShapeDtypeStruct((d_model, width), jnp.bfloat16))


def _chunk_major_out_blocks(w_out, layer, n_steps, n_groups, c):
    rows, d_model = w_out.shape[1:]
    n_types = rows // (n_groups * c)
    blk = rows // n_steps
    assert blk % BF16_SUBLANES == 0 and c % blk == 0
    per_group = c // blk

    def src(s):
        return (layer, jnp.minimum(s, n_steps - 1), 0)

    def dst(s):
        q = jnp.minimum(s, n_steps - 1)
        grp, sub = q // per_group, q % per_group
        return (((grp % n_groups) * n_types + grp // n_groups) * per_group + sub, 0)

    return (pl.BlockSpec((None, blk, d_model), src), pl.BlockSpec((blk, d_model), dst),
            jax.ShapeDtypeStruct((rows, d_model), jnp.bfloat16))


def _mixer_layer(x2, g, win, cw, pw, ps, wo, w_in, w_out, *, layer, seq_len, token_tile):
    n_tok, d_model = x2.shape
    depth, n_groups, c, _ = pw.shape
    d_conv = n_groups * c
    assert seq_len % token_tile == 0 and token_tile >= HALO_ROWS
    n_tiles = n_tok // token_tile
    relayout_next = layer + 1 < depth
    kernel = functools.partial(_mixer_kernel, tiles_per_seq=seq_len // token_tile,
                               relayout_next=relayout_next)
    in_specs = [
        pl.BlockSpec((token_tile, d_model), lambda s: (jnp.minimum(s, n_tiles - 1), 0)),
        _resident((None, 1, d_model), lambda s: (layer, 0, 0)),
        _resident((d_model, 4 * d_conv), lambda s: (0, 0)),
        _resident((None, CONV_WIDTH, d_conv), lambda s: (layer, 0, 0)),
        _resident((None, n_groups, c, c), lambda s: (layer, 0, 0, 0)),
        _resident((None, 1, d_conv), lambda s: (layer, 0, 0)),
        _resident((2 * d_conv, d_model), lambda s: (0, 0)),
    ]
    out_specs = [pl.BlockSpec((token_tile, d_model), lambda s: (jnp.maximum(s - 1, 0), 0))]
    out_shape = [jax.ShapeDtypeStruct((n_tok, d_model), jnp.float32)]
    operands = [x2, g, win, cw, pw, ps, wo]
    if relayout_next:
        for w, blocks in ((w_in, _chunk_major_in_blocks), (w_out, _chunk_major_out_blocks)):
            src_spec, dst_spec, dst_shape = blocks(w, layer + 1, n_tiles, n_groups, c)
            in_specs.append(src_spec)
            out_specs.append(dst_spec)
            out_shape.append(dst_shape)
            operands.append(w)
    outs = pl.pallas_call(
        kernel,
        grid=(n_tiles + 1,),
        in_specs=in_specs,
        out_specs=out_specs,
        out_shape=out_shape,
        scratch_shapes=[
            pltpu.VMEM((token_tile, d_model), jnp.bfloat16),
            pltpu.VMEM((token_tile, d_model), jnp.float32),
            pltpu.VMEM((token_tile, 2 * c), jnp.float32),
            pltpu.VMEM((token_tile, 2 * c), jnp.float32),
            pltpu.VMEM((n_groups, HALO_ROWS, c), jnp.float32),
            pltpu.VMEM((n_groups, HALO_ROWS, c), jnp.float32),
        ],
        compiler_params=pltpu.CompilerParams(
            dimension_semantics=("arbitrary",),
            vmem_limit_bytes=VMEM_LIMIT_BYTES),
        name="mixer",
    )(*operands)
    if relayout_next:
        return outs
    return outs[0], None, None


def _mlp_layer(x2, g, wu, wd, gf, w_up, w_down, *, layer, token_tile, ff_chunk, final_norm):
    n_tok, d_model = x2.shape
    d_ff = wu.shape[-1]
    depth = w_up.shape[0]
    n_tiles, nk = n_tok // token_tile, d_ff // ff_chunk
    cast_next = layer + 1 < depth
    kernel = functools.partial(_mlp_kernel, final_norm=final_norm, cast_next=cast_next)
    in_specs = [
        pl.BlockSpec((token_tile, d_model), lambda i, k: (i, 0)),
        pl.BlockSpec((None, 1, d_model), lambda i, k: (layer, 0, 0)),
        pl.BlockSpec((d_model, ff_chunk), lambda i, k: (0, k)),
        pl.BlockSpec((ff_chunk, d_model), lambda i, k: (k, 0)),
        pl.BlockSpec((1, d_model), lambda i, k: (0, 0)),
    ]
    out_specs = [pl.BlockSpec((token_tile, d_model), lambda i, k: (i, 0))]
    out_shape = [jax.ShapeDtypeStruct((n_tok, d_model), jnp.float32)]
    operands = [x2, g, wu, wd, gf]
    if cast_next:
        n_steps = n_tiles * nk
        for w in (w_up, w_down):
            rows, cols = w.shape[1:]
            blk = max(BF16_SUBLANES, rows // n_steps)
            assert rows % blk == 0 and n_steps % (rows // blk) == 0
            reps = n_steps // (rows // blk)
            in_specs.append(pl.BlockSpec(
                (None, blk, cols), lambda i, k, reps=reps: (layer + 1, (i * nk + k) // reps, 0)))
            out_specs.append(pl.BlockSpec(
                (blk, cols), lambda i, k, reps=reps: ((i * nk + k) // reps, 0)))
            out_shape.append(jax.ShapeDtypeStruct((rows, cols), jnp.bfloat16))
            operands.append(w)
    outs = pl.pallas_call(
        kernel,
        grid=(n_tiles, nk),
        in_specs=in_specs,
        out_specs=out_specs,
        out_shape=out_shape,
        scratch_shapes=[pltpu.VMEM((token_tile, d_model), jnp.bfloat16)],
        compiler_params=pltpu.CompilerParams(
            dimension_semantics=("arbitrary", "arbitrary"),
            vmem_limit_bytes=VMEM_LIMIT_BYTES),
        name="mlp_final" if final_norm else "mlp",
    )(*operands)
    if cast_next:
        return outs
    return outs[0], None, None


def _relayout_kernel(w_ref, o_ref):
    o_ref[...] = w_ref[...].astype(o_ref.dtype)


def _chunk_major_in(w_in, layer, n_groups, c):
    _, d_model, width = w_in.shape
    n_types = width // (n_groups * c)
    return pl.pallas_call(
        _relayout_kernel,
        grid=(n_types, n_groups),
        in_specs=[pl.BlockSpec((None, d_model, c), lambda t, j: (layer, 0, t * n_groups + j))],
        out_specs=pl.BlockSpec((d_model, c), lambda t, j: (0, j * n_types + n_types - 1 - t)),
        out_shape=jax.ShapeDtypeStruct((d_model, width), jnp.bfloat16),
        name="relayout_in",
    )(w_in)


def _chunk_major_out(w_out, layer, n_groups, c):
    _, rows, d_model = w_out.shape
    n_types = rows // (n_groups * c)
    return pl.pallas_call(
        _relayout_kernel,
        grid=(n_types, n_groups),
        in_specs=[pl.BlockSpec((None, c, d_model), lambda t, j: (layer, t * n_groups + j, 0))],
        out_specs=pl.BlockSpec((c, d_model), lambda t, j: (j * n_types + t, 0)),
        out_shape=jax.ShapeDtypeStruct((rows, d_model), jnp.bfloat16),
        name="relayout_out",
    )(w_out)


def kernel(x, w_in, conv_w, pool_w, pool_scale, w_out, norm_mix, norm_mlp, w_up, w_down, norm_final):
    batch, seq_len, d_model = x.shape
    depth = w_in.shape[0]
    n_groups, group_dim = pool_w.shape[1], pool_w.shape[2]
    d_conv = conv_w.shape[-1]
    assert n_groups == len(POOL_WINDOWS) and all(
        w == 2 ** (g + 1) for g, w in enumerate(POOL_WINDOWS))
    assert d_conv == n_groups * group_dim and w_in.shape[-1] == 4 * d_conv
    assert conv_w.shape[1] == CONV_WIDTH

    win = _chunk_major_in(w_in, 0, n_groups, group_dim)
    wo = _chunk_major_out(w_out, 0, n_groups, group_dim)
    pw = pool_w.astype(jnp.bfloat16)
    wu = w_up[0].astype(jnp.bfloat16)
    wd = w_down[0].astype(jnp.bfloat16)
    g_mix = norm_mix.reshape(depth, 1, d_model)
    g_mlp = norm_mlp.reshape(depth, 1, d_model)
    ps = pool_scale.reshape(depth, 1, d_conv)
    gf = norm_final.reshape(1, d_model)

    xs = x.reshape(batch * seq_len, d_model)
    for l in range(depth):
        xs, win_next, wo_next = _mixer_layer(xs, g_mix, win, conv_w, pw, ps, wo, w_in, w_out, layer=l,
                                             seq_len=seq_len,
                                             token_tile=min(MIXER_TOKEN_TILE, seq_len))
        xs, wu, wd = _mlp_layer(xs, g_mlp, wu, wd, gf, w_up, w_down, layer=l,
                                token_tile=min(MLP_TOKEN_TILE, batch * seq_len),
                                ff_chunk=min(MLP_FF_CHUNK, w_up.shape[-1]),
                                final_norm=(l == depth - 1))
        win, wo = win_next, wo_next
    return xs.reshape(batch, seq_len, d_model)
```
